```python
import jax, jax.numpy as jnp
from jax import lax
import numpy as np

D_MODEL = 1024
BATCH = 2
SEQ = 16384
DEPTH = 1
DEC_BATCH = 8
DEC_SEQ = 4096
PAST_LEN = 128

MIX_WIDTH = D_MODEL
MLA_WIDTH = D_MODEL // 2
FNET_WIDTH = MIX_WIDTH - MLA_WIDTH
N_HEADS = 4
V_HEAD_DIM = MLA_WIDTH // N_HEADS
QK_NOPE_DIM = 128
QK_ROPE_DIM = 64
QK_HEAD_DIM = QK_NOPE_DIM + QK_ROPE_DIM
Q_LORA = D_MODEL // 4
KV_LORA = D_MODEL // 8
FNET_GROUPS = 4
FNET_GROUP_DIM = FNET_WIDTH // FNET_GROUPS
D_FF = 2816
IN_COLS = Q_LORA + KV_LORA + QK_ROPE_DIM + FNET_WIDTH
Q_BLOCK = 128
EPS = 1e-6
ROPE_THETA = 10000.0

kernel_name = "hybrid_mla_fnet_macaron_encoder"


def rms_norm(x, g):
    xf = x.astype(jnp.float32)
    y = xf * lax.rsqrt(jnp.mean(xf * xf, axis=-1, keepdims=True) + EPS)
    return (y * g.astype(jnp.float32)).astype(x.dtype)


def swiglu(x, w_gate, w_up, w_down):
    return (jax.nn.silu(x @ w_gate) * (x @ w_up)) @ w_down


def rope_tables(seq):
    inv = 1.0 / (ROPE_THETA ** (jnp.arange(0, QK_ROPE_DIM, 2, dtype=jnp.float32) / QK_ROPE_DIM))
    ang = jnp.arange(seq, dtype=jnp.float32)[:, None] * inv[None, :]
    return jnp.cos(ang), jnp.sin(ang)


def apply_rope(x, cos, sin):
    xf = x.astype(jnp.float32)
    x1, x2 = jnp.split(xf, 2, axis=-1)
    out = jnp.concatenate([x1 * cos - x2 * sin, x2 * cos + x1 * sin], axis=-1)
    return out.astype(x.dtype)


def bidirectional_attention(q, k, v):
    B, S, H, Dq = q.shape
    nb = S // Q_BLOCK
    qb = q.reshape(B, nb, Q_BLOCK, H, Dq).transpose(1, 0, 2, 3, 4)
    scale = Dq ** -0.5

    def one_block(qblk):
        s = jnp.einsum('bqhd,bkhd->bhqk', qblk, k, preferred_element_type=jnp.float32) * scale
        p = jax.nn.softmax(s, axis=-1)
        return jnp.einsum('bhqk,bkhd->bqhd', p.astype(v.dtype), v)

    out = lax.map(one_block, qb)
    return out.transpose(1, 0, 2, 3, 4).reshape(B, S, H, v.shape[-1])


def fourier_mix(u):
    B, S, _ = u.shape
    ug = u.reshape(B, S, FNET_GROUPS, FNET_GROUP_DIM).astype(jnp.float32)
    f = jnp.fft.fft2(ug, axes=(1, 3), norm="ortho")
    return jnp.real(f).reshape(B, S, FNET_WIDTH).astype(u.dtype)


def token_mixing(h, w_in, g_q, w_uq, g_kv, w_ukv, w_o):
    B, S, _ = h.shape
    proj = h @ w_in
    c_q, c_kv, k_r, u = jnp.split(
        proj, [Q_LORA, Q_LORA + KV_LORA, Q_LORA + KV_LORA + QK_ROPE_DIM], axis=-1)
    q = (rms_norm(c_q, g_q) @ w_uq).reshape(B, S, N_HEADS, QK_HEAD_DIM)
    q_nope, q_rope = jnp.split(q, [QK_NOPE_DIM], axis=-1)
    kv = (rms_norm(c_kv, g_kv) @ w_ukv).reshape(B, S, N_HEADS, QK_NOPE_DIM + V_HEAD_DIM)
    k_nope, v = jnp.split(kv, [QK_NOPE_DIM], axis=-1)
    cos, sin = rope_tables(S)
    q_rope = apply_rope(q_rope, cos[:, None, :], sin[:, None, :])
    k_rope = apply_rope(k_r, cos, sin)
    q_full = jnp.concatenate([q_nope, q_rope], axis=-1)
    k_full = jnp.concatenate(
        [k_nope, jnp.broadcast_to(k_rope[:, :, None, :], (B, S, N_HEADS, QK_ROPE_DIM))], axis=-1)
    attn = bidirectional_attention(q_full, k_full, v).reshape(B, S, MLA_WIDTH)
    four = fourier_mix(u)
    return jnp.concatenate([attn, four], axis=-1) @ w_o


def trunk(x, g_ffn1, w1_gate, w1_up, w1_down, g_mix, w_in, g_q, w_uq, g_kv, w_ukv, w_o,
          g_ffn2, w2_gate, w2_up, w2_down, g_final):
    for l in range(DEPTH):
        x = x + 0.5 * swiglu(rms_norm(x, g_ffn1[l]), w1_gate[l], w1_up[l], w1_down[l])
        x = x + token_mixing(rms_norm(x, g_mix[l]), w_in[l], g_q[l], w_uq[l], g_kv[l], w_ukv[l], w_o[l])
        x = x + 0.5 * swiglu(rms_norm(x, g_ffn2[l]), w2_gate[l], w2_up[l], w2_down[l])
    return rms_norm(x, g_final)


def setup_inputs(seed: int = 0) -> dict:
    key = jax.random.key(seed)
    ks = jax.random.split(key, 20)

    def w(k, shape, fan_in):
        return jax.random.normal(k, shape, jnp.float32) * (fan_in ** -0.5)

    def gain(k, shape):
        return 1.0 + 0.01 * jax.random.normal(k, shape, jnp.float32)

    L = DEPTH
    return {
        "x_prompt": jax.random.normal(ks[0], (BATCH, SEQ, D_MODEL), jnp.float32),
        "x_sample": jax.random.normal(ks[1], (DEC_BATCH, DEC_SEQ, D_MODEL), jnp.float32),
        "g_ffn1": gain(ks[2], (L, D_MODEL)),
        "w1_gate": w(ks[3], (L, D_MODEL, D_FF), D_MODEL),
        "w1_up": w(ks[4], (L, D_MODEL, D_FF), D_MODEL),
        "w1_down": w(ks[5], (L, D_FF, D_MODEL), D_FF),
        "g_mix": gain(ks[6], (L, D_MODEL)),
        "w_in": w(ks[7], (L, D_MODEL, IN_COLS), D_MODEL),
        "g_q": gain(ks[8], (L, Q_LORA)),
        "w_uq": w(ks[9], (L, Q_LORA, N_HEADS * QK_HEAD_DIM), Q_LORA),
        "g_kv": gain(ks[10], (L, KV_LORA)),
        "w_ukv": w(ks[11], (L, KV_LORA, N_HEADS * (QK_NOPE_DIM + V_HEAD_DIM)), KV_LORA),
        "w_o": w(ks[12], (L, MIX_WIDTH, D_MODEL), MIX_WIDTH),
        "g_ffn2": gain(ks[13], (L, D_MODEL)),
        "w2_gate": w(ks[14], (L, D_MODEL, D_FF), D_MODEL),
        "w2_up": w(ks[15], (L, D_MODEL, D_FF), D_MODEL),
        "w2_down": w(ks[16], (L, D_FF, D_MODEL), D_FF),
        "g_final": gain(ks[17], (D_MODEL,)),
    }


def reference(x_prompt, x_sample, g_ffn1, w1_gate, w1_up, w1_down, g_mix, w_in, g_q, w_uq,
              g_kv, w_ukv, w_o, g_ffn2, w2_gate, w2_up, w2_down, g_final):
    y_prompt = trunk(x_prompt, g_ffn1, w1_gate, w1_up, w1_down, g_mix, w_in, g_q, w_uq, g_kv,
                     w_ukv, w_o, g_ffn2, w2_gate, w2_up, w2_down, g_final)
    y_sample = trunk(x_sample, g_ffn1, w1_gate, w1_up, w1_down, g_mix, w_in, g_q, w_uq, g_kv,
                     w_ukv, w_o, g_ffn2, w2_gate, w2_up, w2_down, g_final)
    return (y_prompt, y_sample)
```

```python
import functools
import math

import numpy as np
import jax
import jax.numpy as jnp
from jax import lax
from jax.experimental import pallas as pl
from jax.experimental.pallas import tpu as pltpu

D_MODEL = 1024
MLA_WIDTH = D_MODEL // 2
FNET_WIDTH = D_MODEL - MLA_WIDTH
N_HEADS = 4
V_HEAD_DIM = MLA_WIDTH // N_HEADS
QK_NOPE_DIM = 128
QK_ROPE_DIM = 64
QK_HEAD_DIM = QK_NOPE_DIM + QK_ROPE_DIM
Q_LORA = D_MODEL // 4
KV_LORA = D_MODEL // 8
FNET_GROUPS = 4
FNET_GROUP_DIM = FNET_WIDTH // FNET_GROUPS
D_FF = 2816
EPS = 1e-6
ROPE_THETA = 10000.0

LANES = 128
VMEM_LIMIT_BYTES = 56 * 1024 * 1024

F32 = jnp.float32
BF16 = jnp.bfloat16


def _const_spec(shape):
    return pl.BlockSpec(shape, lambda *_: (0,) * len(shape), pipeline_mode=pl.Buffered(1))


def _rms(x, g):
    return x * lax.rsqrt(jnp.mean(x * x, axis=-1, keepdims=True) + EPS) * g


def _swiglu(h, wg_ref, wu_ref, wd_ref):
    g = jnp.dot(h, wg_ref[...], preferred_element_type=F32)
    u = jnp.dot(h, wu_ref[...], preferred_element_type=F32)
    a = (g / (1.0 + jnp.exp(-g))) * u
    return jnp.dot(a.astype(BF16), wd_ref[...], preferred_element_type=F32)


def _ffn1_kernel(x_ref, g_ref, wg_ref, wu_ref, wd_ref, o_ref):
    x = x_ref[...]
    h = _rms(x, g_ref[...]).astype(BF16)
    o_ref[...] = x + 0.5 * _swiglu(h, wg_ref, wu_ref, wd_ref)


def _ffn1(x2d, g, wg, wu, wd, tm):
    t = x2d.shape[0]
    return pl.pallas_call(
        _ffn1_kernel,
        grid=(t // tm,),
        in_specs=[
            pl.BlockSpec((tm, D_MODEL), lambda i: (i, 0)),
            _const_spec((1, D_MODEL)),
            _const_spec((D_MODEL, D_FF)),
            _const_spec((D_MODEL, D_FF)),
            _const_spec((D_FF, D_MODEL)),
        ],
        out_specs=pl.BlockSpec((tm, D_MODEL), lambda i: (i, 0)),
        out_shape=jax.ShapeDtypeStruct((t, D_MODEL), F32),
        compiler_params=pltpu.CompilerParams(
            dimension_semantics=("arbitrary",), vmem_limit_bytes=VMEM_LIMIT_BYTES),
        name="ffn1",
    )(x2d, g, wg, wu, wd)


def _mixpre_kernel(x_ref, gmix_ref, win_ref, gq_ref, wuq_ref, gkv_ref, wukv_ref,
                   tc_ref, ts_ref, cs_ref, qT_ref, k_ref, vT_ref, z_ref, *, n2):
    tm = x_ref.shape[0]
    h = _rms(x_ref[...], gmix_ref[...]).astype(BF16)
    proj = jnp.dot(h, win_ref[...], preferred_element_type=F32)
    tc = tc_ref[...]
    ts = ts_ref[...]

    lane = lax.broadcasted_iota(jnp.int32, (tm, LANES), 1)
    tk = jnp.where(lane < QK_ROPE_DIM, tc, ts)
    t = proj[:, Q_LORA + KV_LORA:Q_LORA + KV_LORA + LANES] * tk
    k_rope = (t[:, :QK_ROPE_DIM] + t[:, QK_ROPE_DIM:]).astype(BF16)

    cq = _rms(proj[:, :Q_LORA], gq_ref[...]).astype(BF16)
    q = jnp.dot(cq, wuq_ref[...], preferred_element_type=F32)
    q = q * (QK_HEAD_DIM ** -0.5 * math.log2(math.e))
    nope_w = N_HEADS * QK_NOPE_DIM
    rope_w = N_HEADS * QK_ROPE_DIM
    for half in range(rope_w // LANES):
        lo = nope_w + half * LANES
        r = q[:, lo:lo + LANES] * tc + q[:, lo + rope_w:lo + rope_w + LANES] * ts
        rT = r.T.astype(BF16)
        for j in range(LANES // QK_ROPE_DIM):
            hd = half * (LANES // QK_ROPE_DIM) + j
            qT_ref[hd, QK_NOPE_DIM:, :] = rT[j * QK_ROPE_DIM:(j + 1) * QK_ROPE_DIM, :]
    for hd in range(N_HEADS):
        qT_ref[hd, :QK_NOPE_DIM, :] = q[:, hd * QK_NOPE_DIM:(hd + 1) * QK_NOPE_DIM].T.astype(BF16)

    ckv = _rms(proj[:, Q_LORA:Q_LORA + KV_LORA], gkv_ref[...]).astype(BF16)
    kv = jnp.dot(ckv, wukv_ref[...], preferred_element_type=F32)
    for hd in range(N_HEADS):
        k_ref[hd, :, :QK_NOPE_DIM] = kv[:, hd * QK_NOPE_DIM:(hd + 1) * QK_NOPE_DIM].astype(BF16)
        k_ref[hd, :, QK_NOPE_DIM:] = k_rope
        v = kv[:, nope_w + hd * V_HEAD_DIM:nope_w + (hd + 1) * V_HEAD_DIM]
        vT_ref[hd, :, :] = v.T.astype(BF16)

    u = proj[:, D_MODEL - FNET_WIDTH:].astype(BF16)
    for grp in range(FNET_GROUPS):
        z = jnp.dot(u[:, grp * FNET_GROUP_DIM:(grp + 1) * FNET_GROUP_DIM], cs_ref[...],
                    preferred_element_type=F32)
        z_ref[grp, 0] = z[:, :FNET_GROUP_DIM].reshape(tm // n2, n2, FNET_GROUP_DIM)
        z_ref[grp, 1] = z[:, FNET_GROUP_DIM:].reshape(tm // n2, n2, FNET_GROUP_DIM)


def _mixpre(x1, b, s, n1, n2, gmix, win, gq, wuq, gkv, wukv, tc, ts, cs, tm):
    t = b * s
    ns = s // tm
    kernel = functools.partial(_mixpre_kernel, n2=n2)
    return pl.pallas_call(
        kernel,
        grid=(t // tm,),
        in_specs=[
            pl.BlockSpec((tm, D_MODEL), lambda i: (i, 0)),
            _const_spec((1, D_MODEL)),
            _const_spec((D_MODEL, D_MODEL)),
            _const_spec((1, Q_LORA)),
            _const_spec((Q_LORA, 2 * MLA_WIDTH)),
            _const_spec((1, KV_LORA)),
            _const_spec((KV_LORA, 2 * MLA_WIDTH)),
            pl.BlockSpec((tm, LANES), lambda i: (i % ns, 0)),
            pl.BlockSpec((tm, LANES), lambda i: (i % ns, 0)),
            _const_spec((FNET_GROUP_DIM, 2 * FNET_GROUP_DIM)),
        ],
        out_specs=[
            pl.BlockSpec((None, N_HEADS, QK_HEAD_DIM, tm), lambda i: (i // ns, 0, 0, i % ns)),
            pl.BlockSpec((None, N_HEADS, tm, QK_HEAD_DIM), lambda i: (i // ns, 0, i % ns, 0)),
            pl.BlockSpec((None, N_HEADS, V_HEAD_DIM, tm), lambda i: (i // ns, 0, 0, i % ns)),
            pl.BlockSpec((None, FNET_GROUPS, 2, tm // n2, n2, FNET_GROUP_DIM),
                         lambda i: (i // ns, 0, 0, i % ns, 0, 0)),
        ],
        out_shape=[
            jax.ShapeDtypeStruct((b, N_HEADS, QK_HEAD_DIM, s), BF16),
            jax.ShapeDtypeStruct((b, N_HEADS, s, QK_HEAD_DIM), BF16),
            jax.ShapeDtypeStruct((b, N_HEADS, V_HEAD_DIM, s), BF16),
            jax.ShapeDtypeStruct((b, FNET_GROUPS, 2, n1, n2, FNET_GROUP_DIM), F32),
        ],
        compiler_params=pltpu.CompilerParams(
            dimension_semantics=("arbitrary",), vmem_limit_bytes=VMEM_LIMIT_BYTES),
        name="mixpre",
    )(x1, gmix, win, gq, wuq, gkv, wukv, tc, ts, cs)


def _attn_kernel(qT_ref, k_ref, vT_ref, o_ref, *, tk):
    qT = qT_ref[...]
    tq = qT.shape[1]
    nk = k_ref.shape[0] // tk

    def body(i, carry):
        m, l, acc = carry
        off = pl.multiple_of(i * tk, tk)
        sT = jnp.dot(k_ref[pl.ds(off, tk), :], qT, preferred_element_type=F32)
        m_new = jnp.maximum(m, jnp.max(sT, axis=0, keepdims=True))
        alpha = jnp.exp2(m - m_new)
        p = jnp.exp2(sT - m_new)
        l = alpha * l + jnp.sum(p, axis=0, keepdims=True)
        pv = jnp.dot(vT_ref[:, pl.ds(off, tk)], p.astype(BF16), preferred_element_type=F32)
        return m_new, l, alpha * acc + pv

    m0 = jnp.full((1, tq), -1e30, F32)
    l0 = jnp.zeros((1, tq), F32)
    acc0 = jnp.zeros((V_HEAD_DIM, tq), F32)
    _, l, acc = lax.fori_loop(0, nk, body, (m0, l0, acc0))
    o_ref[...] = (acc / l).T.astype(o_ref.dtype)


def _attention(qT, k, vT, tq, tk):
    b, _, _, s = qT.shape
    kernel = functools.partial(_attn_kernel, tk=tk)
    return pl.pallas_call(
        kernel,
        grid=(b, N_HEADS, s // tq),
        in_specs=[
            pl.BlockSpec((None, None, QK_HEAD_DIM, tq), lambda bi, hi, qi: (bi, hi, 0, qi)),
            pl.BlockSpec((None, None, s, QK_HEAD_DIM), lambda bi, hi, qi: (bi, hi, 0, 0)),
            pl.BlockSpec((None, None, V_HEAD_DIM, s), lambda bi, hi, qi: (bi, hi, 0, 0)),
        ],
        out_specs=pl.BlockSpec((None, tq, V_HEAD_DIM), lambda bi, hi, qi: (bi, qi, hi)),
        out_shape=jax.ShapeDtypeStruct((b, s, MLA_WIDTH), BF16),
        compiler_params=pltpu.CompilerParams(
            dimension_semantics=("arbitrary", "arbitrary", "arbitrary"),
            vmem_limit_bytes=VMEM_LIMIT_BYTES),
        name="attn",
    )(qT, k, vT)


def _dft1_kernel(z_ref, f_ref, tr_ref, ti_ref, o_ref):
    n1 = z_ref.shape[1]
    nc = z_ref.shape[2]
    f = f_ref[...]
    for j in range(0, nc, 2):
        rhs = jnp.concatenate(
            [jnp.concatenate([z_ref[0, :, j, :], z_ref[0, :, j + 1, :]], axis=1),
             jnp.concatenate([z_ref[1, :, j, :], z_ref[1, :, j + 1, :]], axis=1)], axis=0)
        g = jnp.dot(f, rhs.astype(BF16), preferred_element_type=F32)
        for e in range(2):
            gr = g[:n1, e * LANES:(e + 1) * LANES]
            gi = g[n1:, e * LANES:(e + 1) * LANES]
            tr = tr_ref[j + e]
            ti = ti_ref[j + e]
            o_ref[0, j + e] = gr * tr - gi * ti
            o_ref[1, j + e] = gr * ti + gi * tr


def _dft1(z, fbig, tr, ti, nc):
    bg, _, n1, n2, c = z.shape
    return pl.pallas_call(
        _dft1_kernel,
        grid=(n2 // nc, bg),
        in_specs=[
            pl.BlockSpec((None, 2, n1, nc, c), lambda ci, gi: (gi, 0, 0, ci, 0)),
            _const_spec((2 * n1, 2 * n1)),
            pl.BlockSpec((nc, n1, c), lambda ci, gi: (ci, 0, 0)),
            pl.BlockSpec((nc, n1, c), lambda ci, gi: (ci, 0, 0)),
        ],
        out_specs=pl.BlockSpec((None, 2, nc, n1, c), lambda ci, gi: (gi, 0, ci, 0, 0)),
        out_shape=jax.ShapeDtypeStruct((bg, 2, n2, n1, c), F32),
        compiler_params=pltpu.CompilerParams(
            dimension_semantics=("arbitrary", "arbitrary"), vmem_limit_bytes=VMEM_LIMIT_BYTES),
        name="dft1",
    )(z, fbig, tr, ti)


def _dft2_kernel(w_ref, f_ref, o_ref):
    kc = w_ref.shape[2]
    f = f_ref[...]
    for j in range(0, kc, 2):
        rhs = jnp.concatenate(
            [jnp.concatenate([w_ref[0, :, j, :], w_ref[0, :, j + 1, :]], axis=1),
             jnp.concatenate([w_ref[1, :, j, :], w_ref[1, :, j + 1, :]], axis=1)], axis=0)
        y = jnp.dot(f, rhs.astype(BF16), preferred_element_type=F32)
        o_ref[:, j, :] = y[:, :LANES]
        o_ref[:, j + 1, :] = y[:, LANES:]


def _dft2(w, f2, b, kc):
    bg, _, n2, n1, c = w.shape
    g = bg // b
    return pl.pallas_call(
        _dft2_kernel,
        grid=(bg, n1 // kc),
        in_specs=[
            pl.BlockSpec((None, 2, n2, kc, c), lambda gi, ki: (gi, 0, 0, ki, 0)),
            _const_spec((n2, 2 * n2)),
        ],
        out_specs=pl.BlockSpec((None, n2, kc, c), lambda gi, ki: (gi // g, 0, ki, gi % g)),
        out_shape=jax.ShapeDtypeStruct((b, n2, n1, g * c), F32),
        compiler_params=pltpu.CompilerParams(
            dimension_semantics=("arbitrary", "arbitrary"), vmem_limit_bytes=VMEM_LIMIT_BYTES),
        name="dft2",
    )(w, f2)


def _post_kernel(x_ref, a_ref, f_ref, wo_ref, g2_ref, wg_ref, wu_ref, wd_ref, gf_ref, o_ref):
    mix = jnp.concatenate([a_ref[...], f_ref[...].astype(BF16)], axis=-1)
    x = x_ref[...] + jnp.dot(mix, wo_ref[...], preferred_element_type=F32)
    h = _rms(x, g2_ref[...]).astype(BF16)
    x = x + 0.5 * _swiglu(h, wg_ref, wu_ref, wd_ref)
    o_ref[...] = _rms(x, gf_ref[...])


def _post(x1, attn, four, wo, g2, wg, wu, wd, gf, tm):
    t = x1.shape[0]
    return pl.pallas_call(
        _post_kernel,
        grid=(t // tm,),
        in_specs=[
            pl.BlockSpec((tm, D_MODEL), lambda i: (i, 0)),
            pl.BlockSpec((tm, MLA_WIDTH), lambda i: (i, 0)),
            pl.BlockSpec((tm, FNET_WIDTH), lambda i: (i, 0)),
            _const_spec((D_MODEL, D_MODEL)),
            _const_spec((1, D_MODEL)),
            _const_spec((D_MODEL, D_FF)),
            _const_spec((D_MODEL, D_FF)),
            _const_spec((D_FF, D_MODEL)),
            _const_spec((1, D_MODEL)),
        ],
        out_specs=pl.BlockSpec((tm, D_MODEL), lambda i: (i, 0)),
        out_shape=jax.ShapeDtypeStruct((t, D_MODEL), F32),
        compiler_params=pltpu.CompilerParams(
            dimension_semantics=("arbitrary",), vmem_limit_bytes=VMEM_LIMIT_BYTES),
        name="post",
    )(x1, attn, four, wo, g2, wg, wu, wd, gf)


def _rope_tables(s):
    inv = 1.0 / (ROPE_THETA ** (jnp.arange(0, QK_ROPE_DIM, 2, dtype=F32) / QK_ROPE_DIM))
    ang = jnp.arange(s, dtype=F32)[:, None] * inv[None, :]
    cos, sin = jnp.cos(ang), jnp.sin(ang)
    tc = jnp.tile(cos, (1, LANES // cos.shape[1]))
    ts = jnp.tile(jnp.concatenate([-sin, sin], axis=1), (1, LANES // (2 * sin.shape[1])))
    return tc, ts


def _dft_mats(n):
    k = np.arange(n)
    ang = 2.0 * np.pi * ((k[:, None] * k[None, :]) % n) / n
    return np.cos(ang) / math.sqrt(n), -np.sin(ang) / math.sqrt(n)


def _twiddle(n1, n2):
    s = n1 * n2
    prod = (jnp.arange(n2, dtype=jnp.int32)[:, None] * jnp.arange(n1, dtype=jnp.int32)[None, :]) % s
    ang = prod.astype(F32) * (2.0 * math.pi / s)
    tr = jnp.broadcast_to(jnp.cos(ang)[:, :, None], (n2, n1, LANES))
    ti = jnp.broadcast_to(-jnp.sin(ang)[:, :, None], (n2, n1, LANES))
    return tr, ti


def _split_seq(s):
    n1 = 1 << ((s.bit_length() - 1 + 1) // 2)
    n2 = s // n1
    assert n1 * n2 == s and n2 % 16 == 0 and n1 % 16 == 0
    return n1, n2


def _prep_weights(w_in, w_uq, w_ukv):
    half = QK_ROPE_DIM // 2
    o = Q_LORA + KV_LORA
    k_r = w_in[:, o:o + QK_ROPE_DIM]
    k_sw = jnp.concatenate([k_r[:, half:], k_r[:, :half]], axis=1)
    win = jnp.concatenate([w_in[:, :o], k_r, k_sw, w_in[:, o + QK_ROPE_DIM:]], axis=1)

    wq = w_uq.reshape(Q_LORA, N_HEADS, QK_HEAD_DIM)
    nope = wq[:, :, :QK_NOPE_DIM].reshape(Q_LORA, -1)
    rope = wq[:, :, QK_NOPE_DIM:]
    rope_sw = jnp.concatenate([rope[:, :, half:], rope[:, :, :half]], axis=2)
    wuq = jnp.concatenate([nope, rope.reshape(Q_LORA, -1), rope_sw.reshape(Q_LORA, -1)], axis=1)

    wkv = w_ukv.reshape(KV_LORA, N_HEADS, QK_NOPE_DIM + V_HEAD_DIM)
    wukv = jnp.concatenate([wkv[:, :, :QK_NOPE_DIM].reshape(KV_LORA, -1),
                            wkv[:, :, QK_NOPE_DIM:].reshape(KV_LORA, -1)], axis=1)
    return win.astype(BF16), wuq.astype(BF16), wukv.astype(BF16)


def _trunk(x, w, tm=512, tq=512, tk=512):
    b, s, _ = x.shape
    t = b * s
    n1, n2 = _split_seq(s)
    tm = min(tm, s)
    x2d = x.reshape(t, D_MODEL)

    x1 = _ffn1(x2d, w["g_ffn1"], w["w1_gate"], w["w1_up"], w["w1_down"], tm)

    tc, ts = _rope_tables(s)
    qT, k, vT, z = _mixpre(x1, b, s, n1, n2, w["g_mix"], w["w_in"], w["g_q"], w["w_uq"],
                           w["g_kv"], w["w_ukv"], tc, ts, w["cs"], tm)
    attn = _attention(qT, k, vT, min(tq, s), min(tk, s))

    fr1, fi1 = _dft_mats(n1)
    fbig = jnp.asarray(np.block([[fr1, -fi1], [fi1, fr1]]), F32).astype(BF16)
    fr2, fi2 = _dft_mats(n2)
    f2 = jnp.asarray(np.concatenate([fr2, -fi2], axis=1), F32).astype(BF16)
    tr, ti = _twiddle(n1, n2)
    zz = z.reshape(b * FNET_GROUPS, 2, n1, n2, FNET_GROUP_DIM)
    wst = _dft1(zz, fbig, tr, ti, min(16, n2))
    four = _dft2(wst, f2, b, min(16, n1)).reshape(t, FNET_WIDTH)

    y = _post(x1, attn.reshape(t, MLA_WIDTH), four, w["w_o"], w["g_ffn2"], w["w2_gate"],
              w["w2_up"], w["w2_down"], w["g_final"], tm)
    return y.reshape(b, s, D_MODEL)


def kernel(x_prompt, x_sample, g_ffn1, w1_gate, w1_up, w1_down, g_mix, w_in, g_q, w_uq,
           g_kv, w_ukv, w_o, g_ffn2, w2_gate, w2_up, w2_down, g_final):
    win, wuq, wukv = _prep_weights(w_in[0], w_uq[0], w_ukv[0])
    c = np.arange(FNET_GROUP_DIM)
    ang = 2.0 * np.pi * ((c[:, None] * c[None, :]) % FNET_GROUP_DIM) / FNET_GROUP_DIM
    cs = np.concatenate([np.cos(ang), -np.sin(ang)], axis=1) / math.sqrt(FNET_GROUP_DIM)
    w = {
        "g_ffn1": g_ffn1[0][None, :], "w1_gate": w1_gate[0].astype(BF16),
        "w1_up": w1_up[0].astype(BF16), "w1_down": w1_down[0].astype(BF16),
        "g_mix": g_mix[0][None, :], "w_in": win, "g_q": g_q[0][None, :], "w_uq": wuq,
        "g_kv": g_kv[0][None, :], "w_ukv": wukv, "w_o": w_o[0].astype(BF16),
        "g_ffn2": g_ffn2[0][None, :], "w2_gate": w2_gate[0].astype(BF16),
        "w2_up": w2_up[0].astype(BF16), "w2_down": w2_down[0].astype(BF16),
        "g_final": g_final[None, :], "cs": jnp.asarray(cs, F32).astype(BF16),
    }
    return _trunk(x_prompt, w), _trunk(x_sample, w)
```

```python
import functools
import math

import numpy as np
import jax
import jax.numpy as jnp
from jax import lax
from jax.experimental import pallas as pl
from jax.experimental.pallas import tpu as pltpu

D_MODEL = 1024
MLA_WIDTH = D_MODEL // 2
FNET_WIDTH = D_MODEL - MLA_WIDTH
N_HEADS = 4
V_HEAD_DIM = MLA_WIDTH // N_HEADS
QK_NOPE_DIM = 128
QK_ROPE_DIM = 64
QK_HEAD_DIM = QK_NOPE_DIM + QK_ROPE_DIM
Q_LORA = D_MODEL // 4
KV_LORA = D_MODEL // 8
FNET_GROUPS = 4
FNET_GROUP_DIM = FNET_WIDTH // FNET_GROUPS
D_FF = 2816
EPS = 1e-6
ROPE_THETA = 10000.0

LANES = 128
VMEM_LIMIT_BYTES = 56 * 1024 * 1024

F32 = jnp.float32
BF16 = jnp.bfloat16


def _const_spec(shape):
    return pl.BlockSpec(shape, lambda *_: (0,) * len(shape), pipeline_mode=pl.Buffered(1))


def _rms(x, g):
    return x * lax.rsqrt(jnp.mean(x * x, axis=-1, keepdims=True) + EPS) * g


def _swiglu(h, wg_ref, wu_ref, wd_ref):
    g = jnp.dot(h, wg_ref[...], preferred_element_type=F32)
    u = jnp.dot(h, wu_ref[...], preferred_element_type=F32)
    a = (g / (1.0 + jnp.exp(-g))) * u
    return jnp.dot(a.astype(BF16), wd_ref[...], preferred_element_type=F32)


def _ffn1_kernel(x_ref, g_ref, wg_ref, wu_ref, wd_ref, o_ref):
    x = x_ref[...]
    h = _rms(x, g_ref[...]).astype(BF16)
    o_ref[...] = x + 0.5 * _swiglu(h, wg_ref, wu_ref, wd_ref)


def _ffn1(x2d, g, wg, wu, wd, tm):
    t = x2d.shape[0]
    return pl.pallas_call(
        _ffn1_kernel,
        grid=(t // tm,),
        in_specs=[
            pl.BlockSpec((tm, D_MODEL), lambda i: (i, 0)),
            _const_spec((1, D_MODEL)),
            _const_spec((D_MODEL, D_FF)),
            _const_spec((D_MODEL, D_FF)),
            _const_spec((D_FF, D_MODEL)),
        ],
        out_specs=pl.BlockSpec((tm, D_MODEL), lambda i: (i, 0)),
        out_shape=jax.ShapeDtypeStruct((t, D_MODEL), F32),
        compiler_params=pltpu.CompilerParams(
            dimension_semantics=("arbitrary",), vmem_limit_bytes=VMEM_LIMIT_BYTES),
        name="ffn1",
    )(x2d, g, wg, wu, wd)


def _mixpre_kernel(x_ref, gmix_ref, win_ref, gq_ref, wuq_ref, gkv_ref, wukv_ref,
                   tc_ref, ts_ref, cs_ref, qT_ref, k_ref, vT_ref, z_ref, *, n2):
    tm = x_ref.shape[0]
    h = _rms(x_ref[...], gmix_ref[...]).astype(BF16)
    proj = jnp.dot(h, win_ref[...], preferred_element_type=F32)
    tc = tc_ref[...]
    ts = ts_ref[...]

    lane = lax.broadcasted_iota(jnp.int32, (tm, LANES), 1)
    tk = jnp.where(lane < QK_ROPE_DIM, tc, ts)
    t = proj[:, Q_LORA + KV_LORA:Q_LORA + KV_LORA + LANES] * tk
    k_rope = (t[:, :QK_ROPE_DIM] + t[:, QK_ROPE_DIM:]).astype(BF16)

    cq = _rms(proj[:, :Q_LORA], gq_ref[...]).astype(BF16)
    q = jnp.dot(cq, wuq_ref[...], preferred_element_type=F32)
    q = q * (QK_HEAD_DIM ** -0.5 * math.log2(math.e))
    nope_w = N_HEADS * QK_NOPE_DIM
    rope_w = N_HEADS * QK_ROPE_DIM
    for half in range(rope_w // LANES):
        lo = nope_w + half * LANES
        r = q[:, lo:lo + LANES] * tc + q[:, lo + rope_w:lo + rope_w + LANES] * ts
        rT = r.T.astype(BF16)
        for j in range(LANES // QK_ROPE_DIM):
            hd = half * (LANES // QK_ROPE_DIM) + j
            qT_ref[hd, QK_NOPE_DIM:, :] = rT[j * QK_ROPE_DIM:(j + 1) * QK_ROPE_DIM, :]
    for hd in range(N_HEADS):
        qT_ref[hd, :QK_NOPE_DIM, :] = q[:, hd * QK_NOPE_DIM:(hd + 1) * QK_NOPE_DIM].T.astype(BF16)

    ckv = _rms(proj[:, Q_LORA:Q_LORA + KV_LORA], gkv_ref[...]).astype(BF16)
    kv = jnp.dot(ckv, wukv_ref[...], preferred_element_type=F32)
    for hd in range(N_HEADS):
        k_ref[hd, :, :QK_NOPE_DIM] = kv[:, hd * QK_NOPE_DIM:(hd + 1) * QK_NOPE_DIM].astype(BF16)
        k_ref[hd, :, QK_NOPE_DIM:] = k_rope
        v = kv[:, nope_w + hd * V_HEAD_DIM:nope_w + (hd + 1) * V_HEAD_DIM]
        vT_ref[hd, :, :] = v.T.astype(BF16)

    u = proj[:, D_MODEL - FNET_WIDTH:].astype(BF16)
    for grp in range(FNET_GROUPS):
        z = jnp.dot(u[:, grp * FNET_GROUP_DIM:(grp + 1) * FNET_GROUP_DIM], cs_ref[...],
                    preferred_element_type=F32)
        z_ref[grp, 0] = z[:, :FNET_GROUP_DIM].reshape(tm // n2, n2, FNET_GROUP_DIM)
        z_ref[grp, 1] = z[:, FNET_GROUP_DIM:].reshape(tm // n2, n2, FNET_GROUP_DIM)


def _mixpre(x1, b, s, n1, n2, gmix, win, gq, wuq, gkv, wukv, tc, ts, cs, tm):
    t = b * s
    ns = s // tm
    kernel = functools.partial(_mixpre_kernel, n2=n2)
    return pl.pallas_call(
        kernel,
        grid=(t // tm,),
        in_specs=[
            pl.BlockSpec((tm, D_MODEL), lambda i: (i, 0)),
            _const_spec((1, D_MODEL)),
            _const_spec((D_MODEL, D_MODEL)),
            _const_spec((1, Q_LORA)),
            _const_spec((Q_LORA, 2 * MLA_WIDTH)),
            _const_spec((1, KV_LORA)),
            _const_spec((KV_LORA, 2 * MLA_WIDTH)),
            pl.BlockSpec((tm, LANES), lambda i: (i % ns, 0)),
            pl.BlockSpec((tm, LANES), lambda i: (i % ns, 0)),
            _const_spec((FNET_GROUP_DIM, 2 * FNET_GROUP_DIM)),
        ],
        out_specs=[
            pl.BlockSpec((None, N_HEADS, QK_HEAD_DIM, tm), lambda i: (i // ns, 0, 0, i % ns)),
            pl.BlockSpec((None, N_HEADS, tm, QK_HEAD_DIM), lambda i: (i // ns, 0, i % ns, 0)),
            pl.BlockSpec((None, N_HEADS, V_HEAD_DIM, tm), lambda i: (i // ns, 0, 0, i % ns)),
            pl.BlockSpec((None, FNET_GROUPS, 2, tm // n2, n2, FNET_GROUP_DIM),
                         lambda i: (i // ns, 0, 0, i % ns, 0, 0)),
        ],
        out_shape=[
            jax.ShapeDtypeStruct((b, N_HEADS, QK_HEAD_DIM, s), BF16),
            jax.ShapeDtypeStruct((b, N_HEADS, s, QK_HEAD_DIM), BF16),
            jax.ShapeDtypeStruct((b, N_HEADS, V_HEAD_DIM, s), BF16),
            jax.ShapeDtypeStruct((b, FNET_GROUPS, 2, n1, n2, FNET_GROUP_DIM), F32),
        ],
        compiler_params=pltpu.CompilerParams(
            dimension_semantics=("arbitrary",), vmem_limit_bytes=VMEM_LIMIT_BYTES),
        name="mixpre",
    )(x1, gmix, win, gq, wuq, gkv, wukv, tc, ts, cs)


def _attn_kernel(qT_ref, k_ref, vT_ref, o_ref, s_scr, p_scr, acc_scr, *, tk, unroll):
    tq = qT_ref.shape[1]
    nk = k_ref.shape[0] // tk
    ring = s_scr.shape[0]

    def scores(c, slot):
        off = pl.multiple_of(c * tk, tk)
        s_scr[slot] = jnp.dot(k_ref[pl.ds(off, tk), :], qT_ref[...],
                              preferred_element_type=F32)

    def pv(c, slot, alpha):
        off = pl.multiple_of(c * tk, tk)
        acc_scr[...] = alpha * acc_scr[...] + jnp.dot(
            vT_ref[:, pl.ds(off, tk)], p_scr[slot], preferred_element_type=F32)

    def softmax(slot, m, l):
        m_new = jnp.maximum(m, jnp.max(s_scr[slot], axis=0, keepdims=True))
        alpha = jnp.exp2(m - m_new)
        p = jnp.exp2(s_scr[slot] - m_new)
        p_scr[slot] = p.astype(BF16)
        return m_new, alpha * l + jnp.sum(p, axis=0, keepdims=True), alpha

    def trip(t, carry):
        m, l, alpha_prev = carry
        for j in range(unroll):
            c = t * unroll + j
            slot, nxt, prv = j % ring, (j + 1) % ring, (j - 1) % ring
            scores(jnp.minimum(c + 1, nk - 1), nxt)
            pv(jnp.maximum(c - 1, 0), prv, alpha_prev)
            m, l, alpha_prev = softmax(slot, m, l)
        return m, l, alpha_prev

    acc_scr[...] = jnp.zeros_like(acc_scr)
    p_scr[ring - 1] = jnp.zeros((tk, tq), BF16)
    scores(0, 0)
    init = (jnp.full((1, tq), -1e30, F32), jnp.zeros((1, tq), F32), jnp.ones((1, tq), F32))
    _, l, alpha_last = lax.fori_loop(0, nk // unroll, trip, init)
    pv(nk - 1, (nk - 1) % ring, alpha_last)
    o_ref[...] = (acc_scr[...] / l).T.astype(o_ref.dtype)


def _attention(qT, k, vT, tq, tk):
    b, _, _, s = qT.shape
    nk = s // tk
    unroll = min(8, nk)
    ring = 2
    assert unroll % ring == 0 and nk % unroll == 0
    kernel = functools.partial(_attn_kernel, tk=tk, unroll=unroll)
    return pl.pallas_call(
        kernel,
        grid=(b, N_HEADS, s // tq),
        in_specs=[
            pl.BlockSpec((None, None, QK_HEAD_DIM, tq), lambda bi, hi, qi: (bi, hi, 0, qi)),
            pl.BlockSpec((None, None, s, QK_HEAD_DIM), lambda bi, hi, qi: (bi, hi, 0, 0)),
            pl.BlockSpec((None, None, V_HEAD_DIM, s), lambda bi, hi, qi: (bi, hi, 0, 0)),
        ],
        out_specs=pl.BlockSpec((None, tq, V_HEAD_DIM), lambda bi, hi, qi: (bi, qi, hi)),
        out_shape=jax.ShapeDtypeStruct((b, s, MLA_WIDTH), BF16),
        scratch_shapes=[
            pltpu.VMEM((ring, tk, tq), F32),
            pltpu.VMEM((ring, tk, tq), BF16),
            pltpu.VMEM((V_HEAD_DIM, tq), F32),
        ],
        compiler_params=pltpu.CompilerParams(
            dimension_semantics=("arbitrary", "arbitrary", "arbitrary"),
            vmem_limit_bytes=VMEM_LIMIT_BYTES),
        name="attn",
    )(qT, k, vT)


def _dft1_kernel(z_ref, f_ref, tr_ref, ti_ref, o_ref):
    n1 = z_ref.shape[1]
    nc = z_ref.shape[2]
    f = f_ref[...]
    for j in range(0, nc, 2):
        rhs = jnp.concatenate(
            [jnp.concatenate([z_ref[0, :, j, :], z_ref[0, :, j + 1, :]], axis=1),
             jnp.concatenate([z_ref[1, :, j, :], z_ref[1, :, j + 1, :]], axis=1)], axis=0)
        g = jnp.dot(f, rhs.astype(BF16), preferred_element_type=F32)
        for e in range(2):
            gr = g[:n1, e * LANES:(e + 1) * LANES]
            gi = g[n1:, e * LANES:(e + 1) * LANES]
            tr = tr_ref[j + e]
            ti = ti_ref[j + e]
            o_ref[0, j + e] = gr * tr - gi * ti
            o_ref[1, j + e] = gr * ti + gi * tr


def _dft1(z, fbig, tr, ti, nc):
    bg, _, n1, n2, c = z.shape
    return pl.pallas_call(
        _dft1_kernel,
        grid=(n2 // nc, bg),
        in_specs=[
            pl.BlockSpec((None, 2, n1, nc, c), lambda ci, gi: (gi, 0, 0, ci, 0)),
            _const_spec((2 * n1, 2 * n1)),
            pl.BlockSpec((nc, n1, c), lambda ci, gi: (ci, 0, 0)),
            pl.BlockSpec((nc, n1, c), lambda ci, gi: (ci, 0, 0)),
        ],
        out_specs=pl.BlockSpec((None, 2, nc, n1, c), lambda ci, gi: (gi, 0, ci, 0, 0)),
        out_shape=jax.ShapeDtypeStruct((bg, 2, n2, n1, c), F32),
        compiler_params=pltpu.CompilerParams(
            dimension_semantics=("arbitrary", "arbitrary"), vmem_limit_bytes=VMEM_LIMIT_BYTES),
        name="dft1",
    )(z, fbig, tr, ti)


def _dft2_kernel(w_ref, f_ref, o_ref):
    kc = w_ref.shape[2]
    f = f_ref[...]
    for j in range(0, kc, 2):
        rhs = jnp.concatenate(
            [jnp.concatenate([w_ref[0, :, j, :], w_ref[0, :, j + 1, :]], axis=1),
             jnp.concatenate([w_ref[1, :, j, :], w_ref[1, :, j + 1, :]], axis=1)], axis=0)
        y = jnp.dot(f, rhs.astype(BF16), preferred_element_type=F32)
        o_ref[:, j, :] = y[:, :LANES]
        o_ref[:, j + 1, :] = y[:, LANES:]


def _dft2(w, f2, b, kc):
    bg, _, n2, n1, c = w.shape
    g = bg // b
    return pl.pallas_call(
        _dft2_kernel,
        grid=(bg, n1 // kc),
        in_specs=[
            pl.BlockSpec((None, 2, n2, kc, c), lambda gi, ki: (gi, 0, 0, ki, 0)),
            _const_spec((n2, 2 * n2)),
        ],
        out_specs=pl.BlockSpec((None, n2, kc, c), lambda gi, ki: (gi // g, 0, ki, gi % g)),
        out_shape=jax.ShapeDtypeStruct((b, n2, n1, g * c), F32),
        compiler_params=pltpu.CompilerParams(
            dimension_semantics=("arbitrary", "arbitrary"), vmem_limit_bytes=VMEM_LIMIT_BYTES),
        name="dft2",
    )(w, f2)


def _post_kernel(x_ref, a_ref, f_ref, wo_ref, g2_ref, wg_ref, wu_ref, wd_ref, gf_ref, o_ref):
    mix = jnp.concatenate([a_ref[...], f_ref[...].astype(BF16)], axis=-1)
    x = x_ref[...] + jnp.dot(mix, wo_ref[...], preferred_element_type=F32)
    h = _rms(x, g2_ref[...]).astype(BF16)
    x = x + 0.5 * _swiglu(h, wg_ref, wu_ref, wd_ref)
    o_ref[...] = _rms(x, gf_ref[...])


def _post(x1, attn, four, wo, g2, wg, wu, wd, gf, tm):
    t = x1.shape[0]
    return pl.pallas_call(
        _post_kernel,
        grid=(t // tm,),
        in_specs=[
            pl.BlockSpec((tm, D_MODEL), lambda i: (i, 0)),
            pl.BlockSpec((tm, MLA_WIDTH), lambda i: (i, 0)),
            pl.BlockSpec((tm, FNET_WIDTH), lambda i: (i, 0)),
            _const_spec((D_MODEL, D_MODEL)),
            _const_spec((1, D_MODEL)),
            _const_spec((D_MODEL, D_FF)),
            _const_spec((D_MODEL, D_FF)),
            _const_spec((D_FF, D_MODEL)),
            _const_spec((1, D_MODEL)),
        ],
        out_specs=pl.BlockSpec((tm, D_MODEL), lambda i: (i, 0)),
        out_shape=jax.ShapeDtypeStruct((t, D_MODEL), F32),
        compiler_params=pltpu.CompilerParams(
            dimension_semantics=("arbitrary",), vmem_limit_bytes=VMEM_LIMIT_BYTES),
        name="post",
    )(x1, attn, four, wo, g2, wg, wu, wd, gf)


def _rope_tables(s):
    inv = 1.0 / (ROPE_THETA ** (jnp.arange(0, QK_ROPE_DIM, 2, dtype=F32) / QK_ROPE_DIM))
    ang = jnp.arange(s, dtype=F32)[:, None] * inv[None, :]
    cos, sin = jnp.cos(ang), jnp.sin(ang)
    tc = jnp.tile(cos, (1, LANES // cos.shape[1]))
    ts = jnp.tile(jnp.concatenate([-sin, sin], axis=1), (1, LANES // (2 * sin.shape[1])))
    return tc, ts


def _dft_mats(n):
    k = np.arange(n)
    ang = 2.0 * np.pi * ((k[:, None] * k[None, :]) % n) / n
    return np.cos(ang) / math.sqrt(n), -np.sin(ang) / math.sqrt(n)


def _twiddle(n1, n2):
    s = n1 * n2
    prod = (jnp.arange(n2, dtype=jnp.int32)[:, None] * jnp.arange(n1, dtype=jnp.int32)[None, :]) % s
    ang = prod.astype(F32) * (2.0 * math.pi / s)
    tr = jnp.broadcast_to(jnp.cos(ang)[:, :, None], (n2, n1, LANES))
    ti = jnp.broadcast_to(-jnp.sin(ang)[:, :, None], (n2, n1, LANES))
    return tr, ti


def _split_seq(s):
    n1 = 1 << ((s.bit_length() - 1 + 1) // 2)
    n2 = s // n1
    assert n1 * n2 == s and n2 % 16 == 0 and n1 % 16 == 0
    return n1, n2


def _prep_weights(w_in, w_uq, w_ukv):
    half = QK_ROPE_DIM // 2
    o = Q_LORA + KV_LORA
    k_r = w_in[:, o:o + QK_ROPE_DIM]
    k_sw = jnp.concatenate([k_r[:, half:], k_r[:, :half]], axis=1)
    win = jnp.concatenate([w_in[:, :o], k_r, k_sw, w_in[:, o + QK_ROPE_DIM:]], axis=1)

    wq = w_uq.reshape(Q_LORA, N_HEADS, QK_HEAD_DIM)
    nope = wq[:, :, :QK_NOPE_DIM].reshape(Q_LORA, -1)
    rope = wq[:, :, QK_NOPE_DIM:]
    rope_sw = jnp.concatenate([rope[:, :, half:], rope[:, :, :half]], axis=2)
    wuq = jnp.concatenate([nope, rope.reshape(Q_LORA, -1), rope_sw.reshape(Q_LORA, -1)], axis=1)

    wkv = w_ukv.reshape(KV_LORA, N_HEADS, QK_NOPE_DIM + V_HEAD_DIM)
    wukv = jnp.concatenate([wkv[:, :, :QK_NOPE_DIM].reshape(KV_LORA, -1),
                            wkv[:, :, QK_NOPE_DIM:].reshape(KV_LORA, -1)], axis=1)
    return win.astype(BF16), wuq.astype(BF16), wukv.astype(BF16)


def _trunk(x, w, tm=512, tq=512, tk=512):
    b, s, _ = x.shape
    t = b * s
    n1, n2 = _split_seq(s)
    tm = min(tm, s)
    x2d = x.reshape(t, D_MODEL)

    x1 = _ffn1(x2d, w["g_ffn1"], w["w1_gate"], w["w1_up"], w["w1_down"], tm)

    tc, ts = _rope_tables(s)
    qT, k, vT, z = _mixpre(x1, b, s, n1, n2, w["g_mix"], w["w_in"], w["g_q"], w["w_uq"],
                           w["g_kv"], w["w_ukv"], tc, ts, w["cs"], tm)
    attn = _attention(qT, k, vT, min(tq, s), min(tk, s))

    fr1, fi1 = _dft_mats(n1)
    fbig = jnp.asarray(np.block([[fr1, -fi1], [fi1, fr1]]), F32).astype(BF16)
    fr2, fi2 = _dft_mats(n2)
    f2 = jnp.asarray(np.concatenate([fr2, -fi2], axis=1), F32).astype(BF16)
    tr, ti = _twiddle(n1, n2)
    zz = z.reshape(b * FNET_GROUPS, 2, n1, n2, FNET_GROUP_DIM)
    wst = _dft1(zz, fbig, tr, ti, min(16, n2))
    four = _dft2(wst, f2, b, min(16, n1)).reshape(t, FNET_WIDTH)

    y = _post(x1, attn.reshape(t, MLA_WIDTH), four, w["w_o"], w["g_ffn2"], w["w2_gate"],
              w["w2_up"], w["w2_down"], w["g_final"], tm)
    return y.reshape(b, s, D_MODEL)


def kernel(x_prompt, x_sample, g_ffn1, w1_gate, w1_up, w1_down, g_mix, w_in, g_q, w_uq,
           g_kv, w_ukv, w_o, g_ffn2, w2_gate, w2_up, w2_down, g_final):
    win, wuq, wukv = _prep_weights(w_in[0], w_uq[0], w_ukv[0])
    c = np.arange(FNET_GROUP_DIM)
    ang = 2.0 * np.pi * ((c[:, None] * c[None, :]) % FNET_GROUP_DIM) / FNET_GROUP_DIM
    cs = np.concatenate([np.cos(ang), -np.sin(ang)], axis=1) / math.sqrt(FNET_GROUP_DIM)
    w = {
        "g_ffn1": g_ffn1[0][None, :], "w1_gate": w1_gate[0].astype(BF16),
        "w1_up": w1_up[0].astype(BF16), "w1_down": w1_down[0].astype(BF16),
        "g_mix": g_mix[0][None, :], "w_in": win, "g_q": g_q[0][None, :], "w_uq": wuq,
        "g_kv": g_kv[0][None, :], "w_ukv": wukv, "w_o": w_o[0].astype(BF16),
        "g_ffn2": g_ffn2[0][None, :], "w2_gate": w2_gate[0].astype(BF16),
        "w2_up": w2_up[0].astype(BF16), "w2_down": w2_down[0].astype(BF16),
        "g_final": g_final[None, :], "cs": jnp.asarray(cs, F32).astype(BF16),
    }
    return _trunk(x_prompt, w), _trunk(x_sample, w)
```

```python
import functools
import math

import numpy as np
import jax
import jax.numpy as jnp
from jax import lax
from jax.experimental import pallas as pl
from jax.experimental.pallas import tpu as pltpu

D_MODEL = 1024
MLA_WIDTH = D_MODEL // 2
FNET_WIDTH = D_MODEL - MLA_WIDTH
N_HEADS = 4
V_HEAD_DIM = MLA_WIDTH // N_HEADS
QK_NOPE_DIM = 128
QK_ROPE_DIM = 64
QK_HEAD_DIM = QK_NOPE_DIM + QK_ROPE_DIM
Q_LORA = D_MODEL // 4
KV_LORA = D_MODEL // 8
FNET_GROUPS = 4
FNET_GROUP_DIM = FNET_WIDTH // FNET_GROUPS
D_FF = 2816
EPS = 1e-6
ROPE_THETA = 10000.0

LANES = 128
BF16_SUBLANES = 16
V_ROWS = V_HEAD_DIM + BF16_SUBLANES
VMEM_LIMIT_BYTES = 56 * 1024 * 1024

F32 = jnp.float32
BF16 = jnp.bfloat16


def _const_spec(shape):
    return pl.BlockSpec(shape, lambda *_: (0,) * len(shape), pipeline_mode=pl.Buffered(1))


def _rms(x, g):
    return x * lax.rsqrt(jnp.mean(x * x, axis=-1, keepdims=True) + EPS) * g


def _swiglu(h, wg_ref, wu_ref, wd_ref):
    g = jnp.dot(h, wg_ref[...], preferred_element_type=F32)
    u = jnp.dot(h, wu_ref[...], preferred_element_type=F32)
    a = (g / (1.0 + jnp.exp(-g))) * u
    return jnp.dot(a.astype(BF16), wd_ref[...], preferred_element_type=F32)


def _ffn1_kernel(x_ref, g_ref, wg_ref, wu_ref, wd_ref, o_ref):
    x = x_ref[...]
    h = _rms(x, g_ref[...]).astype(BF16)
    o_ref[...] = x + 0.5 * _swiglu(h, wg_ref, wu_ref, wd_ref)


def _ffn1(x2d, g, wg, wu, wd, tm):
    t = x2d.shape[0]
    return pl.pallas_call(
        _ffn1_kernel,
        grid=(t // tm,),
        in_specs=[
            pl.BlockSpec((tm, D_MODEL), lambda i: (i, 0)),
            _const_spec((1, D_MODEL)),
            _const_spec((D_MODEL, D_FF)),
            _const_spec((D_MODEL, D_FF)),
            _const_spec((D_FF, D_MODEL)),
        ],
        out_specs=pl.BlockSpec((tm, D_MODEL), lambda i: (i, 0)),
        out_shape=jax.ShapeDtypeStruct((t, D_MODEL), F32),
        compiler_params=pltpu.CompilerParams(
            dimension_semantics=("arbitrary",), vmem_limit_bytes=VMEM_LIMIT_BYTES),
        name="ffn1",
    )(x2d, g, wg, wu, wd)


def _mixpre_kernel(x_ref, gmix_ref, win_ref, gq_ref, wuq_ref, gkv_ref, wukv_ref,
                   tc_ref, ts_ref, cs_ref, qT_ref, k_ref, vT_ref, z_ref, *, n2):
    tm = x_ref.shape[0]
    h = _rms(x_ref[...], gmix_ref[...]).astype(BF16)
    proj = jnp.dot(h, win_ref[...], preferred_element_type=F32)
    tc = tc_ref[...]
    ts = ts_ref[...]

    lane = lax.broadcasted_iota(jnp.int32, (tm, LANES), 1)
    tk = jnp.where(lane < QK_ROPE_DIM, tc, ts)
    t = proj[:, Q_LORA + KV_LORA:Q_LORA + KV_LORA + LANES] * tk
    k_rope = (t[:, :QK_ROPE_DIM] + t[:, QK_ROPE_DIM:]).astype(BF16)

    cq = _rms(proj[:, :Q_LORA], gq_ref[...]).astype(BF16)
    q = jnp.dot(cq, wuq_ref[...], preferred_element_type=F32)
    q = q * (QK_HEAD_DIM ** -0.5 * math.log2(math.e))
    nope_w = N_HEADS * QK_NOPE_DIM
    rope_w = N_HEADS * QK_ROPE_DIM
    for half in range(rope_w // LANES):
        lo = nope_w + half * LANES
        r = q[:, lo:lo + LANES] * tc + q[:, lo + rope_w:lo + rope_w + LANES] * ts
        rT = r.T.astype(BF16)
        for j in range(LANES // QK_ROPE_DIM):
            hd = half * (LANES // QK_ROPE_DIM) + j
            qT_ref[hd, QK_NOPE_DIM:, :] = rT[j * QK_ROPE_DIM:(j + 1) * QK_ROPE_DIM, :]
    for hd in range(N_HEADS):
        qT_ref[hd, :QK_NOPE_DIM, :] = q[:, hd * QK_NOPE_DIM:(hd + 1) * QK_NOPE_DIM].T.astype(BF16)

    ckv = _rms(proj[:, Q_LORA:Q_LORA + KV_LORA], gkv_ref[...]).astype(BF16)
    kv = jnp.dot(ckv, wukv_ref[...], preferred_element_type=F32)
    row = lax.broadcasted_iota(jnp.int32, (BF16_SUBLANES, tm), 0)
    ones_row = jnp.where(row == 0, 1.0, 0.0).astype(BF16)
    for hd in range(N_HEADS):
        k_ref[hd, :, :QK_NOPE_DIM] = kv[:, hd * QK_NOPE_DIM:(hd + 1) * QK_NOPE_DIM].astype(BF16)
        k_ref[hd, :, QK_NOPE_DIM:] = k_rope
        v = kv[:, nope_w + hd * V_HEAD_DIM:nope_w + (hd + 1) * V_HEAD_DIM]
        vT_ref[hd, :V_HEAD_DIM, :] = v.T.astype(BF16)
        vT_ref[hd, V_HEAD_DIM:, :] = ones_row

    u = proj[:, D_MODEL - FNET_WIDTH:].astype(BF16)
    for grp in range(FNET_GROUPS):
        z = jnp.dot(u[:, grp * FNET_GROUP_DIM:(grp + 1) * FNET_GROUP_DIM], cs_ref[...],
                    preferred_element_type=F32)
        z_ref[grp, 0] = z[:, :FNET_GROUP_DIM].reshape(tm // n2, n2, FNET_GROUP_DIM)
        z_ref[grp, 1] = z[:, FNET_GROUP_DIM:].reshape(tm // n2, n2, FNET_GROUP_DIM)


def _mixpre(x1, b, s, n1, n2, gmix, win, gq, wuq, gkv, wukv, tc, ts, cs, tm):
    t = b * s
    ns = s // tm
    kernel = functools.partial(_mixpre_kernel, n2=n2)
    return pl.pallas_call(
        kernel,
        grid=(t // tm,),
        in_specs=[
            pl.BlockSpec((tm, D_MODEL), lambda i: (i, 0)),
            _const_spec((1, D_MODEL)),
            _const_spec((D_MODEL, D_MODEL)),
            _const_spec((1, Q_LORA)),
            _const_spec((Q_LORA, 2 * MLA_WIDTH)),
            _const_spec((1, KV_LORA)),
            _const_spec((KV_LORA, 2 * MLA_WIDTH)),
            pl.BlockSpec((tm, LANES), lambda i: (i % ns, 0)),
            pl.BlockSpec((tm, LANES), lambda i: (i % ns, 0)),
            _const_spec((FNET_GROUP_DIM, 2 * FNET_GROUP_DIM)),
        ],
        out_specs=[
            pl.BlockSpec((None, N_HEADS, QK_HEAD_DIM, tm), lambda i: (i // ns, 0, 0, i % ns)),
            pl.BlockSpec((None, N_HEADS, tm, QK_HEAD_DIM), lambda i: (i // ns, 0, i % ns, 0)),
            pl.BlockSpec((None, N_HEADS, V_ROWS, tm), lambda i: (i // ns, 0, 0, i % ns)),
            pl.BlockSpec((None, FNET_GROUPS, 2, tm // n2, n2, FNET_GROUP_DIM),
                         lambda i: (i // ns, 0, 0, i % ns, 0, 0)),
        ],
        out_shape=[
            jax.ShapeDtypeStruct((b, N_HEADS, QK_HEAD_DIM, s), BF16),
            jax.ShapeDtypeStruct((b, N_HEADS, s, QK_HEAD_DIM), BF16),
            jax.ShapeDtypeStruct((b, N_HEADS, V_ROWS, s), BF16),
            jax.ShapeDtypeStruct((b, FNET_GROUPS, 2, n1, n2, FNET_GROUP_DIM), F32),
        ],
        compiler_params=pltpu.CompilerParams(
            dimension_semantics=("arbitrary",), vmem_limit_bytes=VMEM_LIMIT_BYTES),
        name="mixpre",
    )(x1, gmix, win, gq, wuq, gkv, wukv, tc, ts, cs)


def _attn_kernel(qT_ref, k_ref, vT_ref, o_ref, s_scr, p_scr, acc_scr, *, tk, unroll):
    tq = qT_ref.shape[1]
    nk = k_ref.shape[0] // tk

    def scores(c):
        off = pl.multiple_of(c * tk, tk)
        return jnp.dot(k_ref[pl.ds(off, tk), :], qT_ref[...], preferred_element_type=F32)

    def pv(c, p, alpha):
        off = pl.multiple_of(c * tk, tk)
        acc_scr[...] = alpha * acc_scr[...] + jnp.dot(
            vT_ref[:, pl.ds(off, tk)], p, preferred_element_type=F32)

    def softmax(sv, m):
        m_new = jnp.maximum(m, jnp.max(sv, axis=0, keepdims=True))
        return jnp.exp2((sv - m_new).astype(BF16)), m_new, jnp.exp2(m - m_new)

    def trip(t, carry):
        m, alpha_prev = carry
        sv = s_scr[...]
        p_prev = p_scr[...]
        for j in range(unroll):
            c = t * unroll + j
            s_next = scores(jnp.minimum(c + 1, nk - 1))
            pv(jnp.maximum(c - 1, 0), p_prev, alpha_prev)
            p_prev, m, alpha_prev = softmax(sv, m)
            sv = s_next
        s_scr[...] = sv
        p_scr[...] = p_prev
        return m, alpha_prev

    acc_scr[...] = jnp.zeros_like(acc_scr)
    p_scr[...] = jnp.zeros_like(p_scr)
    s_scr[...] = scores(0)
    init = (jnp.full((1, tq), -1e30, F32), jnp.ones((1, tq), F32))
    _, alpha_last = lax.fori_loop(0, nk // unroll, trip, init)
    pv(nk - 1, p_scr[...], alpha_last)
    l = acc_scr[V_HEAD_DIM:V_HEAD_DIM + 1, :]
    o_ref[...] = (acc_scr[:V_HEAD_DIM, :] / l).T.astype(o_ref.dtype)


def _attention(qT, k, vT, tq, tk):
    b, _, _, s = qT.shape
    nk = s // tk
    unroll = min(8, nk)
    assert nk % unroll == 0
    kernel = functools.partial(_attn_kernel, tk=tk, unroll=unroll)
    return pl.pallas_call(
        kernel,
        grid=(b, N_HEADS, s // tq),
        in_specs=[
            pl.BlockSpec((None, None, QK_HEAD_DIM, tq), lambda bi, hi, qi: (bi, hi, 0, qi)),
            pl.BlockSpec((None, None, s, QK_HEAD_DIM), lambda bi, hi, qi: (bi, hi, 0, 0)),
            pl.BlockSpec((None, None, V_ROWS, s), lambda bi, hi, qi: (bi, hi, 0, 0)),
        ],
        out_specs=pl.BlockSpec((None, tq, V_HEAD_DIM), lambda bi, hi, qi: (bi, qi, hi)),
        out_shape=jax.ShapeDtypeStruct((b, s, MLA_WIDTH), BF16),
        scratch_shapes=[
            pltpu.VMEM((tk, tq), F32),
            pltpu.VMEM((tk, tq), BF16),
            pltpu.VMEM((V_ROWS, tq), F32),
        ],
        compiler_params=pltpu.CompilerParams(
            dimension_semantics=("arbitrary", "arbitrary", "arbitrary"),
            vmem_limit_bytes=VMEM_LIMIT_BYTES),
        name="attn",
    )(qT, k, vT)


def _dft1_kernel(z_ref, f_ref, tr_ref, ti_ref, o_ref):
    n1 = z_ref.shape[1]
    nc = z_ref.shape[2]
    f = f_ref[...]
    for j in range(0, nc, 2):
        rhs = jnp.concatenate(
            [jnp.concatenate([z_ref[0, :, j, :], z_ref[0, :, j + 1, :]], axis=1),
             jnp.concatenate([z_ref[1, :, j, :], z_ref[1, :, j + 1, :]], axis=1)], axis=0)
        g = jnp.dot(f, rhs.astype(BF16), preferred_element_type=F32)
        for e in range(2):
            gr = g[:n1, e * LANES:(e + 1) * LANES]
            gi = g[n1:, e * LANES:(e + 1) * LANES]
            tr = tr_ref[j + e]
            ti = ti_ref[j + e]
            o_ref[0, j + e] = gr * tr - gi * ti
            o_ref[1, j + e] = gr * ti + gi * tr


def _dft1(z, fbig, tr, ti, nc):
    bg, _, n1, n2, c = z.shape
    return pl.pallas_call(
        _dft1_kernel,
        grid=(n2 // nc, bg),
        in_specs=[
            pl.BlockSpec((None, 2, n1, nc, c), lambda ci, gi: (gi, 0, 0, ci, 0)),
            _const_spec((2 * n1, 2 * n1)),
            pl.BlockSpec((nc, n1, c), lambda ci, gi: (ci, 0, 0)),
            pl.BlockSpec((nc, n1, c), lambda ci, gi: (ci, 0, 0)),
        ],
        out_specs=pl.BlockSpec((None, 2, nc, n1, c), lambda ci, gi: (gi, 0, ci, 0, 0)),
        out_shape=jax.ShapeDtypeStruct((bg, 2, n2, n1, c), F32),
        compiler_params=pltpu.CompilerParams(
            dimension_semantics=("arbitrary", "arbitrary"), vmem_limit_bytes=VMEM_LIMIT_BYTES),
        name="dft1",
    )(z, fbig, tr, ti)


def _dft2_kernel(w_ref, f_ref, o_ref):
    kc = w_ref.shape[2]
    f = f_ref[...]
    for j in range(0, kc, 2):
        rhs = jnp.concatenate(
            [jnp.concatenate([w_ref[0, :, j, :], w_ref[0, :, j + 1, :]], axis=1),
             jnp.concatenate([w_ref[1, :, j, :], w_ref[1, :, j + 1, :]], axis=1)], axis=0)
        y = jnp.dot(f, rhs.astype(BF16), preferred_element_type=F32)
        o_ref[:, j, :] = y[:, :LANES]
        o_ref[:, j + 1, :] = y[:, LANES:]


def _dft2(w, f2, b, kc):
    bg, _, n2, n1, c = w.shape
    g = bg // b
    return pl.pallas_call(
        _dft2_kernel,
        grid=(bg, n1 // kc),
        in_specs=[
            pl.BlockSpec((None, 2, n2, kc, c), lambda gi, ki: (gi, 0, 0, ki, 0)),
            _const_spec((n2, 2 * n2)),
        ],
        out_specs=pl.BlockSpec((None, n2, kc, c), lambda gi, ki: (gi // g, 0, ki, gi % g)),
        out_shape=jax.ShapeDtypeStruct((b, n2, n1, g * c), F32),
        compiler_params=pltpu.CompilerParams(
            dimension_semantics=("arbitrary", "arbitrary"), vmem_limit_bytes=VMEM_LIMIT_BYTES),
        name="dft2",
    )(w, f2)


def _post_kernel(x_ref, a_ref, f_ref, wo_ref, g2_ref, wg_ref, wu_ref, wd_ref, gf_ref, o_ref):
    mix = jnp.concatenate([a_ref[...], f_ref[...].astype(BF16)], axis=-1)
    x = x_ref[...] + jnp.dot(mix, wo_ref[...], preferred_element_type=F32)
    h = _rms(x, g2_ref[...]).astype(BF16)
    x = x + 0.5 * _swiglu(h, wg_ref, wu_ref, wd_ref)
    o_ref[...] = _rms(x, gf_ref[...])


def _post(x1, attn, four, wo, g2, wg, wu, wd, gf, tm):
    t = x1.shape[0]
    return pl.pallas_call(
        _post_kernel,
        grid=(t // tm,),
        in_specs=[
            pl.BlockSpec((tm, D_MODEL), lambda i: (i, 0)),
            pl.BlockSpec((tm, MLA_WIDTH), lambda i: (i, 0)),
            pl.BlockSpec((tm, FNET_WIDTH), lambda i: (i, 0)),
            _const_spec((D_MODEL, D_MODEL)),
            _const_spec((1, D_MODEL)),
            _const_spec((D_MODEL, D_FF)),
            _const_spec((D_MODEL, D_FF)),
            _const_spec((D_FF, D_MODEL)),
            _const_spec((1, D_MODEL)),
        ],
        out_specs=pl.BlockSpec((tm, D_MODEL), lambda i: (i, 0)),
        out_shape=jax.ShapeDtypeStruct((t, D_MODEL), F32),
        compiler_params=pltpu.CompilerParams(
            dimension_semantics=("arbitrary",), vmem_limit_bytes=VMEM_LIMIT_BYTES),
        name="post",
    )(x1, attn, four, wo, g2, wg, wu, wd, gf)


def _rope_tables(s):
    inv = 1.0 / (ROPE_THETA ** (jnp.arange(0, QK_ROPE_DIM, 2, dtype=F32) / QK_ROPE_DIM))
    ang = jnp.arange(s, dtype=F32)[:, None] * inv[None, :]
    cos, sin = jnp.cos(ang), jnp.sin(ang)
    tc = jnp.tile(cos, (1, LANES // cos.shape[1]))
    ts = jnp.tile(jnp.concatenate([-sin, sin], axis=1), (1, LANES // (2 * sin.shape[1])))
    return tc, ts


def _dft_mats(n):
    k = np.arange(n)
    ang = 2.0 * np.pi * ((k[:, None] * k[None, :]) % n) / n
    return np.cos(ang) / math.sqrt(n), -np.sin(ang) / math.sqrt(n)


def _twiddle(n1, n2):
    s = n1 * n2
    prod = (jnp.arange(n2, dtype=jnp.int32)[:, None] * jnp.arange(n1, dtype=jnp.int32)[None, :]) % s
    ang = prod.astype(F32) * (2.0 * math.pi / s)
    tr = jnp.broadcast_to(jnp.cos(ang)[:, :, None], (n2, n1, LANES))
    ti = jnp.broadcast_to(-jnp.sin(ang)[:, :, None], (n2, n1, LANES))
    return tr, ti


def _split_seq(s):
    n1 = 1 << ((s.bit_length() - 1 + 1) // 2)
    n2 = s // n1
    assert n1 * n2 == s and n2 % 16 == 0 and n1 % 16 == 0
    return n1, n2


def _prep_weights(w_in, w_uq, w_ukv):
    half = QK_ROPE_DIM // 2
    o = Q_LORA + KV_LORA
    k_r = w_in[:, o:o + QK_ROPE_DIM]
    k_sw = jnp.concatenate([k_r[:, half:], k_r[:, :half]], axis=1)
    win = jnp.concatenate([w_in[:, :o], k_r, k_sw, w_in[:, o + QK_ROPE_DIM:]], axis=1)

    wq = w_uq.reshape(Q_LORA, N_HEADS, QK_HEAD_DIM)
    nope = wq[:, :, :QK_NOPE_DIM].reshape(Q_LORA, -1)
    rope = wq[:, :, QK_NOPE_DIM:]
    rope_sw = jnp.concatenate([rope[:, :, half:], rope[:, :, :half]], axis=2)
    wuq = jnp.concatenate([nope, rope.reshape(Q_LORA, -1), rope_sw.reshape(Q_LORA, -1)], axis=1)

    wkv = w_ukv.reshape(KV_LORA, N_HEADS, QK_NOPE_DIM + V_HEAD_DIM)
    wukv = jnp.concatenate([wkv[:, :, :QK_NOPE_DIM].reshape(KV_LORA, -1),
                            wkv[:, :, QK_NOPE_DIM:].reshape(KV_LORA, -1)], axis=1)
    return win.astype(BF16), wuq.astype(BF16), wukv.astype(BF16)


def _trunk(x, w, tm=512, tq=512, tk=512):
    b, s, _ = x.shape
    t = b * s
    n1, n2 = _split_seq(s)
    tm = min(tm, s)
    x2d = x.reshape(t, D_MODEL)

    x1 = _ffn1(x2d, w["g_ffn1"], w["w1_gate"], w["w1_up"], w["w1_down"], tm)

    tc, ts = _rope_tables(s)
    qT, k, vT, z = _mixpre(x1, b, s, n1, n2, w["g_mix"], w["w_in"], w["g_q"], w["w_uq"],
                           w["g_kv"], w["w_ukv"], tc, ts, w["cs"], tm)
    attn = _attention(qT, k, vT, min(tq, s), min(tk, s))

    fr1, fi1 = _dft_mats(n1)
    fbig = jnp.asarray(np.block([[fr1, -fi1], [fi1, fr1]]), F32).astype(BF16)
    fr2, fi2 = _dft_mats(n2)
    f2 = jnp.asarray(np.concatenate([fr2, -fi2], axis=1), F32).astype(BF16)
    tr, ti = _twiddle(n1, n2)
    zz = z.reshape(b * FNET_GROUPS, 2, n1, n2, FNET_GROUP_DIM)
    wst = _dft1(zz, fbig, tr, ti, min(16, n2))
    four = _dft2(wst, f2, b, min(16, n1)).reshape(t, FNET_WIDTH)

    y = _post(x1, attn.reshape(t, MLA_WIDTH), four, w["w_o"], w["g_ffn2"], w["w2_gate"],
              w["w2_up"], w["w2_down"], w["g_final"], tm)
    return y.reshape(b, s, D_MODEL)


def kernel(x_prompt, x_sample, g_ffn1, w1_gate, w1_up, w1_down, g_mix, w_in, g_q, w_uq,
           g_kv, w_ukv, w_o, g_ffn2, w2_gate, w2_up, w2_down, g_final):
    win, wuq, wukv = _prep_weights(w_in[0], w_uq[0], w_ukv[0])
    c = np.arange(FNET_GROUP_DIM)
    ang = 2.0 * np.pi * ((c[:, None] * c[None, :]) % FNET_GROUP_DIM) / FNET_GROUP_DIM
    cs = np.concatenate([np.cos(ang), -np.sin(ang)], axis=1) / math.sqrt(FNET_GROUP_DIM)
    w = {
        "g_ffn1": g_ffn1[0][None, :], "w1_gate": w1_gate[0].astype(BF16),
        "w1_up": w1_up[0].astype(BF16), "w1_down": w1_down[0].astype(BF16),
        "g_mix": g_mix[0][None, :], "w_in": win, "g_q": g_q[0][None, :], "w_uq": wuq,
        "g_kv": g_kv[0][None, :], "w_ukv": wukv, "w_o": w_o[0].astype(BF16),
        "g_ffn2": g_ffn2[0][None, :], "w2_gate": w2_gate[0].astype(BF16),
        "w2_up": w2_up[0].astype(BF16), "w2_down": w2_down[0].astype(BF16),
        "g_final": g_final[None, :], "cs": jnp.asarray(cs, F32).astype(BF16),
    }
    return _trunk(x_prompt, w), _trunk(x_sample, w)
```

```python
import functools
import math

import numpy as np
import jax
import jax.numpy as jnp
from jax import lax
from jax.experimental import pallas as pl
from jax.experimental.pallas import tpu as pltpu

D_MODEL = 1024
MLA_WIDTH = D_MODEL // 2
FNET_WIDTH = D_MODEL - MLA_WIDTH
N_HEADS = 4
V_HEAD_DIM = MLA_WIDTH // N_HEADS
QK_NOPE_DIM = 128
QK_ROPE_DIM = 64
QK_HEAD_DIM = QK_NOPE_DIM + QK_ROPE_DIM
Q_LORA = D_MODEL // 4
KV_LORA = D_MODEL // 8
FNET_GROUPS = 4
FNET_GROUP_DIM = FNET_WIDTH // FNET_GROUPS
D_FF = 2816
EPS = 1e-6
ROPE_THETA = 10000.0

LANES = 128
F32_SUBLANES = 8
DFT_CHUNK = 16
DFT_PITCH_PAD = 8
VMEM_LIMIT_BYTES = 56 * 1024 * 1024

F32 = jnp.float32
BF16 = jnp.bfloat16


def _const_spec(shape):
    return pl.BlockSpec(shape, lambda *_: (0,) * len(shape), pipeline_mode=pl.Buffered(1))


def _rms(x, g):
    return x * lax.rsqrt(jnp.mean(x * x, axis=-1, keepdims=True) + EPS) * g


def _swiglu(h, wg_ref, wu_ref, wd_ref):
    g = jnp.dot(h, wg_ref[...], preferred_element_type=F32)
    u = jnp.dot(h, wu_ref[...], preferred_element_type=F32)
    a = (g / (1.0 + jnp.exp(-g))) * u
    return jnp.dot(a.astype(BF16), wd_ref[...], preferred_element_type=F32)


def _ffn1_kernel(x_ref, g_ref, wg_ref, wu_ref, wd_ref, o_ref):
    x = x_ref[...]
    h = _rms(x, g_ref[...]).astype(BF16)
    o_ref[...] = x + 0.5 * _swiglu(h, wg_ref, wu_ref, wd_ref)


def _ffn1(x2d, g, wg, wu, wd, tm):
    t = x2d.shape[0]
    return pl.pallas_call(
        _ffn1_kernel,
        grid=(t // tm,),
        in_specs=[
            pl.BlockSpec((tm, D_MODEL), lambda i: (i, 0)),
            _const_spec((1, D_MODEL)),
            _const_spec((D_MODEL, D_FF)),
            _const_spec((D_MODEL, D_FF)),
            _const_spec((D_FF, D_MODEL)),
        ],
        out_specs=pl.BlockSpec((tm, D_MODEL), lambda i: (i, 0)),
        out_shape=jax.ShapeDtypeStruct((t, D_MODEL), F32),
        compiler_params=pltpu.CompilerParams(
            dimension_semantics=("arbitrary",), vmem_limit_bytes=VMEM_LIMIT_BYTES),
        name="ffn1",
    )(x2d, g, wg, wu, wd)


def _mixpre_kernel(x_ref, gmix_ref, win_ref, gq_ref, wuq_ref, gkv_ref, wukv_ref,
                   tc_ref, ts_ref, cs_ref, qT_ref, k_ref, vT_ref, z_ref, *, n2):
    tm = x_ref.shape[0]
    h = _rms(x_ref[...], gmix_ref[...]).astype(BF16)
    proj = jnp.dot(h, win_ref[...], preferred_element_type=F32)
    tc = tc_ref[...]
    ts = ts_ref[...]

    lane = lax.broadcasted_iota(jnp.int32, (tm, LANES), 1)
    tk = jnp.where(lane < QK_ROPE_DIM, tc, ts)
    t = proj[:, Q_LORA + KV_LORA:Q_LORA + KV_LORA + LANES] * tk
    k_rope = (t[:, :QK_ROPE_DIM] + t[:, QK_ROPE_DIM:]).astype(BF16)

    cq = _rms(proj[:, :Q_LORA], gq_ref[...]).astype(BF16)
    q = jnp.dot(cq, wuq_ref[...], preferred_element_type=F32)
    q = q * (QK_HEAD_DIM ** -0.5 * math.log2(math.e))
    nope_w = N_HEADS * QK_NOPE_DIM
    rope_w = N_HEADS * QK_ROPE_DIM
    for half in range(rope_w // LANES):
        lo = nope_w + half * LANES
        r = q[:, lo:lo + LANES] * tc + q[:, lo + rope_w:lo + rope_w + LANES] * ts
        rT = r.T.astype(BF16)
        for j in range(LANES // QK_ROPE_DIM):
            hd = half * (LANES // QK_ROPE_DIM) + j
            qT_ref[hd, QK_NOPE_DIM:, :] = rT[j * QK_ROPE_DIM:(j + 1) * QK_ROPE_DIM, :]
    for hd in range(N_HEADS):
        qT_ref[hd, :QK_NOPE_DIM, :] = q[:, hd * QK_NOPE_DIM:(hd + 1) * QK_NOPE_DIM].T.astype(BF16)

    ckv = _rms(proj[:, Q_LORA:Q_LORA + KV_LORA], gkv_ref[...]).astype(BF16)
    kv = jnp.dot(ckv, wukv_ref[...], preferred_element_type=F32)
    for hd in range(N_HEADS):
        k_ref[hd, :, :QK_NOPE_DIM] = kv[:, hd * QK_NOPE_DIM:(hd + 1) * QK_NOPE_DIM].astype(BF16)
        k_ref[hd, :, QK_NOPE_DIM:] = k_rope
        v = kv[:, nope_w + hd * V_HEAD_DIM:nope_w + (hd + 1) * V_HEAD_DIM]
        vT_ref[hd, :, :] = v.T.astype(BF16)

    u = proj[:, D_MODEL - FNET_WIDTH:].astype(BF16)
    for grp in range(FNET_GROUPS):
        z = jnp.dot(u[:, grp * FNET_GROUP_DIM:(grp + 1) * FNET_GROUP_DIM], cs_ref[...],
                    preferred_element_type=F32)
        z_ref[grp, 0] = z[:, :FNET_GROUP_DIM].reshape(tm // n2, n2, FNET_GROUP_DIM)
        z_ref[grp, 1] = z[:, FNET_GROUP_DIM:].reshape(tm // n2, n2, FNET_GROUP_DIM)


def _mixpre(x1, b, s, n1, n2, gmix, win, gq, wuq, gkv, wukv, tc, ts, cs, tm):
    t = b * s
    ns = s // tm
    kernel = functools.partial(_mixpre_kernel, n2=n2)
    return pl.pallas_call(
        kernel,
        grid=(t // tm,),
        in_specs=[
            pl.BlockSpec((tm, D_MODEL), lambda i: (i, 0)),
            _const_spec((1, D_MODEL)),
            _const_spec((D_MODEL, D_MODEL)),
            _const_spec((1, Q_LORA)),
            _const_spec((Q_LORA, 2 * MLA_WIDTH)),
            _const_spec((1, KV_LORA)),
            _const_spec((KV_LORA, 2 * MLA_WIDTH)),
            pl.BlockSpec((tm, LANES), lambda i: (i % ns, 0)),
            pl.BlockSpec((tm, LANES), lambda i: (i % ns, 0)),
            _const_spec((FNET_GROUP_DIM, 2 * FNET_GROUP_DIM)),
        ],
        out_specs=[
            pl.BlockSpec((None, N_HEADS, QK_HEAD_DIM, tm), lambda i: (i // ns, 0, 0, i % ns)),
            pl.BlockSpec((None, N_HEADS, tm, QK_HEAD_DIM), lambda i: (i // ns, 0, i % ns, 0)),
            pl.BlockSpec((None, N_HEADS, V_HEAD_DIM, tm), lambda i: (i // ns, 0, 0, i % ns)),
            pl.BlockSpec((None, FNET_GROUPS, 2, tm // n2, n2, FNET_GROUP_DIM),
                         lambda i: (i // ns, 0, 0, i % ns, 0, 0)),
        ],
        out_shape=[
            jax.ShapeDtypeStruct((b, N_HEADS, QK_HEAD_DIM, s), BF16),
            jax.ShapeDtypeStruct((b, N_HEADS, s, QK_HEAD_DIM), BF16),
            jax.ShapeDtypeStruct((b, N_HEADS, V_HEAD_DIM, s), BF16),
            jax.ShapeDtypeStruct((b, FNET_GROUPS, 2, n1, n2, FNET_GROUP_DIM), F32),
        ],
        compiler_params=pltpu.CompilerParams(
            dimension_semantics=("arbitrary",), vmem_limit_bytes=VMEM_LIMIT_BYTES),
        name="mixpre",
    )(x1, gmix, win, gq, wuq, gkv, wukv, tc, ts, cs)


def _attn_kernel(qT_ref, k_ref, vT_ref, o_ref, s_scr, p_scr, acc_scr, *, tk, unroll):
    tq = qT_ref.shape[1]
    nk = k_ref.shape[0] // tk
    ring = s_scr.shape[0]

    def scores(c, slot):
        off = pl.multiple_of(c * tk, tk)
        s_scr[slot] = jnp.dot(k_ref[pl.ds(off, tk), :], qT_ref[...],
                              preferred_element_type=F32)

    def pv(c, slot, alpha):
        off = pl.multiple_of(c * tk, tk)
        acc_scr[...] = alpha * acc_scr[...] + jnp.dot(
            vT_ref[:, pl.ds(off, tk)], p_scr[slot], preferred_element_type=F32)

    def softmax(slot, m, l):
        m_new = jnp.maximum(m, jnp.max(s_scr[slot], axis=0, keepdims=True))
        alpha = jnp.exp2(m - m_new)
        p = jnp.exp2(s_scr[slot] - m_new)
        p_scr[slot] = p.astype(BF16)
        return m_new, alpha * l + jnp.sum(p, axis=0, keepdims=True), alpha

    def trip(t, carry):
        m, l, alpha_prev = carry
        for j in range(unroll):
            c = t * unroll + j
            slot, nxt, prv = j % ring, (j + 1) % ring, (j - 1) % ring
            scores(jnp.minimum(c + 1, nk - 1), nxt)
            pv(jnp.maximum(c - 1, 0), prv, alpha_prev)
            m, l, alpha_prev = softmax(slot, m, l)
        return m, l, alpha_prev

    acc_scr[...] = jnp.zeros_like(acc_scr)
    p_scr[ring - 1] = jnp.zeros((tk, tq), BF16)
    scores(0, 0)
    init = (jnp.full((1, tq), -1e30, F32), jnp.zeros((1, tq), F32), jnp.ones((1, tq), F32))
    _, l, alpha_last = lax.fori_loop(0, nk // unroll, trip, init)
    pv(nk - 1, (nk - 1) % ring, alpha_last)
    o_ref[...] = (acc_scr[...] / l).T.astype(o_ref.dtype)


def _attention(qT, k, vT, tq, tk):
    b, _, _, s = qT.shape
    nk = s // tk
    unroll = min(8, nk)
    ring = 2
    assert unroll % ring == 0 and nk % unroll == 0
    kernel = functools.partial(_attn_kernel, tk=tk, unroll=unroll)
    return pl.pallas_call(
        kernel,
        grid=(b, N_HEADS, s // tq),
        in_specs=[
            pl.BlockSpec((None, None, QK_HEAD_DIM, tq), lambda bi, hi, qi: (bi, hi, 0, qi)),
            pl.BlockSpec((None, None, s, QK_HEAD_DIM), lambda bi, hi, qi: (bi, hi, 0, 0)),
            pl.BlockSpec((None, None, V_HEAD_DIM, s), lambda bi, hi, qi: (bi, hi, 0, 0)),
        ],
        out_specs=pl.BlockSpec((None, tq, V_HEAD_DIM), lambda bi, hi, qi: (bi, qi, hi)),
        out_shape=jax.ShapeDtypeStruct((b, s, MLA_WIDTH), BF16),
        scratch_shapes=[
            pltpu.VMEM((ring, tk, tq), F32),
            pltpu.VMEM((ring, tk, tq), BF16),
            pltpu.VMEM((V_HEAD_DIM, tq), F32),
        ],
        compiler_params=pltpu.CompilerParams(
            dimension_semantics=("arbitrary", "arbitrary", "arbitrary"),
            vmem_limit_bytes=VMEM_LIMIT_BYTES),
        name="attn",
    )(qT, k, vT)


def _dft_kernel(z_ref, f1_ref, tr_ref, ti_ref, f2_ref, o_ref, w_scr, z2, o2, *, steps1):
    step = pl.program_id(1)
    n1, nc = z_ref.shape[1], z_ref.shape[2]
    n2, kc = o_ref.shape[0], o_ref.shape[1]
    pitch = w_scr.shape[0] // (2 * n1)
    tile = F32_SUBLANES

    @pl.when(step < steps1)
    def _():
        base = step * nc
        for grp in range(nc // tile):
            z2[grp] = z_ref[:, :, grp * tile:(grp + 1) * tile, :].reshape(2 * n1 * tile, LANES)
        for j in range(0, nc, 2):
            grp, jl = divmod(j, tile)
            slabs = [[z2[grp, pl.ds(e * n1 * tile + jl + d, n1, stride=tile), :] for d in range(2)]
                     for e in range(2)]
            rhs = jnp.concatenate([jnp.concatenate(slabs[0], axis=1),
                                   jnp.concatenate(slabs[1], axis=1)], axis=0).astype(BF16)
            g = jnp.dot(f1_ref[...], rhs, preferred_element_type=F32)
            for d in range(2):
                gr = g[:n1, d * LANES:(d + 1) * LANES]
                gi = g[n1:, d * LANES:(d + 1) * LANES]
                tr = tr_ref[j + d]
                ti = ti_ref[j + d]
                w_scr[pl.ds(base + j + d, n1, stride=pitch), :] = gr * tr - gi * ti
                w_scr[pl.ds(n1 * pitch + base + j + d, n1, stride=pitch), :] = gr * ti + gi * tr

    @pl.when(step >= steps1)
    def _():
        kb = (step - steps1) * kc
        for j in range(0, kc, 2):
            grp, jl = divmod(j, tile)
            slabs = [[w_scr[pl.ds(pl.multiple_of((e * n1 + kb + j + d) * pitch, tile), n2), :]
                      for d in range(2)] for e in range(2)]
            rhs = jnp.concatenate([jnp.concatenate(slabs[0], axis=1),
                                   jnp.concatenate(slabs[1], axis=1)], axis=0).astype(BF16)
            y = jnp.dot(f2_ref[...], rhs, preferred_element_type=F32)
            o2[grp, pl.ds(jl, n2, stride=tile), :] = y[:, :LANES]
            o2[grp, pl.ds(jl + 1, n2, stride=tile), :] = y[:, LANES:]
        for grp in range(kc // tile):
            o_ref[:, grp * tile:(grp + 1) * tile, :] = o2[grp].reshape(n2, tile, LANES)


def _dft(z, fbig, tr, ti, f2, b):
    bg, _, n1, n2, c = z.shape
    g = bg // b
    nc = kc = DFT_CHUNK
    steps1, steps2 = n2 // nc, n1 // kc
    pitch = n2 + DFT_PITCH_PAD
    kernel = functools.partial(_dft_kernel, steps1=steps1)
    return pl.pallas_call(
        kernel,
        grid=(bg, steps1 + steps2),
        in_specs=[
            pl.BlockSpec((None, 2, n1, nc, c),
                         lambda gi, si: (gi, 0, 0, jnp.minimum(si, steps1 - 1), 0)),
            _const_spec((2 * n1, 2 * n1)),
            pl.BlockSpec((nc, n1, c), lambda gi, si: (jnp.minimum(si, steps1 - 1), 0, 0)),
            pl.BlockSpec((nc, n1, c), lambda gi, si: (jnp.minimum(si, steps1 - 1), 0, 0)),
            _const_spec((n2, 2 * n2)),
        ],
        out_specs=pl.BlockSpec(
            (None, n2, kc, c), lambda gi, si: (gi // g, 0, jnp.maximum(si - steps1, 0), gi % g)),
        out_shape=jax.ShapeDtypeStruct((b, n2, n1, g * c), F32),
        scratch_shapes=[pltpu.VMEM((2 * n1 * pitch, c), F32),
                        pltpu.VMEM((nc // F32_SUBLANES, 2 * n1 * F32_SUBLANES, c), F32),
                        pltpu.VMEM((kc // F32_SUBLANES, n2 * F32_SUBLANES, c), F32)],
        compiler_params=pltpu.CompilerParams(
            dimension_semantics=("arbitrary", "arbitrary"), vmem_limit_bytes=VMEM_LIMIT_BYTES),
        name="dft",
    )(z, fbig, tr, ti, f2)


def _post_kernel(x_ref, a_ref, f_ref, wo_ref, g2_ref, wg_ref, wu_ref, wd_ref, gf_ref, o_ref):
    mix = jnp.concatenate([a_ref[...], f_ref[...].astype(BF16)], axis=-1)
    x = x_ref[...] + jnp.dot(mix, wo_ref[...], preferred_element_type=F32)
    h = _rms(x, g2_ref[...]).astype(BF16)
    x = x + 0.5 * _swiglu(h, wg_ref, wu_ref, wd_ref)
    o_ref[...] = _rms(x, gf_ref[...])


def _post(x1, attn, four, wo, g2, wg, wu, wd, gf, tm):
    t = x1.shape[0]
    return pl.pallas_call(
        _post_kernel,
        grid=(t // tm,),
        in_specs=[
            pl.BlockSpec((tm, D_MODEL), lambda i: (i, 0)),
            pl.BlockSpec((tm, MLA_WIDTH), lambda i: (i, 0)),
            pl.BlockSpec((tm, FNET_WIDTH), lambda i: (i, 0)),
            _const_spec((D_MODEL, D_MODEL)),
            _const_spec((1, D_MODEL)),
            _const_spec((D_MODEL, D_FF)),
            _const_spec((D_MODEL, D_FF)),
            _const_spec((D_FF, D_MODEL)),
            _const_spec((1, D_MODEL)),
        ],
        out_specs=pl.BlockSpec((tm, D_MODEL), lambda i: (i, 0)),
        out_shape=jax.ShapeDtypeStruct((t, D_MODEL), F32),
        compiler_params=pltpu.CompilerParams(
            dimension_semantics=("arbitrary",), vmem_limit_bytes=VMEM_LIMIT_BYTES),
        name="post",
    )(x1, attn, four, wo, g2, wg, wu, wd, gf)


def _rope_tables(s):
    inv = 1.0 / (ROPE_THETA ** (jnp.arange(0, QK_ROPE_DIM, 2, dtype=F32) / QK_ROPE_DIM))
    ang = jnp.arange(s, dtype=F32)[:, None] * inv[None, :]
    cos, sin = jnp.cos(ang), jnp.sin(ang)
    tc = jnp.tile(cos, (1, LANES // cos.shape[1]))
    ts = jnp.tile(jnp.concatenate([-sin, sin], axis=1), (1, LANES // (2 * sin.shape[1])))
    return tc, ts


def _dft_mats(n):
    k = np.arange(n)
    ang = 2.0 * np.pi * ((k[:, None] * k[None, :]) % n) / n
    return np.cos(ang) / math.sqrt(n), -np.sin(ang) / math.sqrt(n)


def _twiddle(n1, n2):
    s = n1 * n2
    prod = (jnp.arange(n2, dtype=jnp.int32)[:, None] * jnp.arange(n1, dtype=jnp.int32)[None, :]) % s
    ang = prod.astype(F32) * (2.0 * math.pi / s)
    tr = jnp.broadcast_to(jnp.cos(ang)[:, :, None], (n2, n1, LANES))
    ti = jnp.broadcast_to(-jnp.sin(ang)[:, :, None], (n2, n1, LANES))
    return tr, ti


def _split_seq(s):
    n1 = 1 << ((s.bit_length() - 1 + 1) // 2)
    n2 = s // n1
    assert n1 * n2 == s and n2 % 16 == 0 and n1 % 16 == 0
    return n1, n2


def _prep_weights(w_in, w_uq, w_ukv):
    half = QK_ROPE_DIM // 2
    o = Q_LORA + KV_LORA
    k_r = w_in[:, o:o + QK_ROPE_DIM]
    k_sw = jnp.concatenate([k_r[:, half:], k_r[:, :half]], axis=1)
    win = jnp.concatenate([w_in[:, :o], k_r, k_sw, w_in[:, o + QK_ROPE_DIM:]], axis=1)

    wq = w_uq.reshape(Q_LORA, N_HEADS, QK_HEAD_DIM)
    nope = wq[:, :, :QK_NOPE_DIM].reshape(Q_LORA, -1)
    rope = wq[:, :, QK_NOPE_DIM:]
    rope_sw = jnp.concatenate([rope[:, :, half:], rope[:, :, :half]], axis=2)
    wuq = jnp.concatenate([nope, rope.reshape(Q_LORA, -1), rope_sw.reshape(Q_LORA, -1)], axis=1)

    wkv = w_ukv.reshape(KV_LORA, N_HEADS, QK_NOPE_DIM + V_HEAD_DIM)
    wukv = jnp.concatenate([wkv[:, :, :QK_NOPE_DIM].reshape(KV_LORA, -1),
                            wkv[:, :, QK_NOPE_DIM:].reshape(KV_LORA, -1)], axis=1)
    return win.astype(BF16), wuq.astype(BF16), wukv.astype(BF16)


def _trunk(x, w, tm=512, tq=512, tk=512):
    b, s, _ = x.shape
    t = b * s
    n1, n2 = _split_seq(s)
    tm = min(tm, s)
    x2d = x.reshape(t, D_MODEL)

    x1 = _ffn1(x2d, w["g_ffn1"], w["w1_gate"], w["w1_up"], w["w1_down"], tm)

    tc, ts = _rope_tables(s)
    qT, k, vT, z = _mixpre(x1, b, s, n1, n2, w["g_mix"], w["w_in"], w["g_q"], w["w_uq"],
                           w["g_kv"], w["w_ukv"], tc, ts, w["cs"], tm)
    attn = _attention(qT, k, vT, min(tq, s), min(tk, s))

    fr1, fi1 = _dft_mats(n1)
    fbig = jnp.asarray(np.block([[fr1, -fi1], [fi1, fr1]]), F32).astype(BF16)
    fr2, fi2 = _dft_mats(n2)
    f2 = jnp.asarray(np.concatenate([fr2, -fi2], axis=1), F32).astype(BF16)
    tr, ti = _twiddle(n1, n2)
    zz = z.reshape(b * FNET_GROUPS, 2, n1, n2, FNET_GROUP_DIM)
    four = _dft(zz, fbig, tr, ti, f2, b).reshape(t, FNET_WIDTH)

    y = _post(x1, attn.reshape(t, MLA_WIDTH), four, w["w_o"], w["g_ffn2"], w["w2_gate"],
              w["w2_up"], w["w2_down"], w["g_final"], tm)
    return y.reshape(b, s, D_MODEL)


def kernel(x_prompt, x_sample, g_ffn1, w1_gate, w1_up, w1_down, g_mix, w_in, g_q, w_uq,
           g_kv, w_ukv, w_o, g_ffn2, w2_gate, w2_up, w2_down, g_final):
    win, wuq, wukv = _prep_weights(w_in[0], w_uq[0], w_ukv[0])
    c = np.arange(FNET_GROUP_DIM)
    ang = 2.0 * np.pi * ((c[:, None] * c[None, :]) % FNET_GROUP_DIM) / FNET_GROUP_DIM
    cs = np.concatenate([np.cos(ang), -np.sin(ang)], axis=1) / math.sqrt(FNET_GROUP_DIM)
    w = {
        "g_ffn1": g_ffn1[0][None, :], "w1_gate": w1_gate[0].astype(BF16),
        "w1_up": w1_up[0].astype(BF16), "w1_down": w1_down[0].astype(BF16),
        "g_mix": g_mix[0][None, :], "w_in": win, "g_q": g_q[0][None, :], "w_uq": wuq,
        "g_kv": g_kv[0][None, :], "w_ukv": wukv, "w_o": w_o[0].astype(BF16),
        "g_ffn2": g_ffn2[0][None, :], "w2_gate": w2_gate[0].astype(BF16),
        "w2_up": w2_up[0].astype(BF16), "w2_down": w2_down[0].astype(BF16),
        "g_final": g_final[None, :], "cs": jnp.asarray(cs, F32).astype(BF16),
    }
    return _trunk(x_prompt, w), _trunk(x_sample, w)
```

```python
import functools
import math

import numpy as np
import jax
import jax.numpy as jnp
from jax import lax
from jax.experimental import pallas as pl
from jax.experimental.pallas import tpu as pltpu

D_MODEL = 1024
MLA_WIDTH = D_MODEL // 2
FNET_WIDTH = D_MODEL - MLA_WIDTH
N_HEADS = 4
V_HEAD_DIM = MLA_WIDTH // N_HEADS
QK_NOPE_DIM = 128
QK_ROPE_DIM = 64
QK_HEAD_DIM = QK_NOPE_DIM + QK_ROPE_DIM
Q_LORA = D_MODEL // 4
KV_LORA = D_MODEL // 8
FNET_GROUPS = 4
FNET_GROUP_DIM = FNET_WIDTH // FNET_GROUPS
D_FF = 2816
EPS = 1e-6
ROPE_THETA = 10000.0

LANES = 128
F32_SUBLANES = 8
DFT_CHUNK = 32
DFT_PITCH_PAD = 8
VMEM_LIMIT_BYTES = 56 * 1024 * 1024

F32 = jnp.float32
BF16 = jnp.bfloat16


def _const_spec(shape):
    return pl.BlockSpec(shape, lambda *_: (0,) * len(shape), pipeline_mode=pl.Buffered(1))


def _rms(x, g):
    return x * lax.rsqrt(jnp.mean(x * x, axis=-1, keepdims=True) + EPS) * g


def _swiglu(h, wg_ref, wu_ref, wd_ref):
    g = jnp.dot(h, wg_ref[...], preferred_element_type=F32)
    u = jnp.dot(h, wu_ref[...], preferred_element_type=F32)
    a = (g / (1.0 + jnp.exp(-g))) * u
    return jnp.dot(a.astype(BF16), wd_ref[...], preferred_element_type=F32)


def _ffn1_kernel(x_ref, g_ref, wg_ref, wu_ref, wd_ref, o_ref):
    x = x_ref[...]
    h = _rms(x, g_ref[...]).astype(BF16)
    o_ref[...] = x + 0.5 * _swiglu(h, wg_ref, wu_ref, wd_ref)


def _ffn1(x2d, g, wg, wu, wd, tm):
    t = x2d.shape[0]
    return pl.pallas_call(
        _ffn1_kernel,
        grid=(t // tm,),
        in_specs=[
            pl.BlockSpec((tm, D_MODEL), lambda i: (i, 0)),
            _const_spec((1, D_MODEL)),
            _const_spec((D_MODEL, D_FF)),
            _const_spec((D_MODEL, D_FF)),
            _const_spec((D_FF, D_MODEL)),
        ],
        out_specs=pl.BlockSpec((tm, D_MODEL), lambda i: (i, 0)),
        out_shape=jax.ShapeDtypeStruct((t, D_MODEL), F32),
        compiler_params=pltpu.CompilerParams(
            dimension_semantics=("arbitrary",), vmem_limit_bytes=VMEM_LIMIT_BYTES),
        name="ffn1",
    )(x2d, g, wg, wu, wd)


def _mixpre_kernel(x_ref, gmix_ref, win_ref, gq_ref, wuq_ref, gkv_ref, wukv_ref,
                   tc_ref, ts_ref, cs_ref, qT_ref, k_ref, vT_ref, z_ref, *, n2):
    tm = x_ref.shape[0]
    h = _rms(x_ref[...], gmix_ref[...]).astype(BF16)
    proj = jnp.dot(h, win_ref[...], preferred_element_type=F32)
    tc = tc_ref[...]
    ts = ts_ref[...]

    lane = lax.broadcasted_iota(jnp.int32, (tm, LANES), 1)
    tk = jnp.where(lane < QK_ROPE_DIM, tc, ts)
    t = proj[:, Q_LORA + KV_LORA:Q_LORA + KV_LORA + LANES] * tk
    k_rope = (t[:, :QK_ROPE_DIM] + t[:, QK_ROPE_DIM:]).astype(BF16)

    cq = _rms(proj[:, :Q_LORA], gq_ref[...]).astype(BF16)
    q = jnp.dot(cq, wuq_ref[...], preferred_element_type=F32)
    q = q * (QK_HEAD_DIM ** -0.5 * math.log2(math.e))
    nope_w = N_HEADS * QK_NOPE_DIM
    rope_w = N_HEADS * QK_ROPE_DIM
    for half in range(rope_w // LANES):
        lo = nope_w + half * LANES
        r = q[:, lo:lo + LANES] * tc + q[:, lo + rope_w:lo + rope_w + LANES] * ts
        rT = r.T.astype(BF16)
        for j in range(LANES // QK_ROPE_DIM):
            hd = half * (LANES // QK_ROPE_DIM) + j
            qT_ref[hd, QK_NOPE_DIM:, :] = rT[j * QK_ROPE_DIM:(j + 1) * QK_ROPE_DIM, :]
    for hd in range(N_HEADS):
        qT_ref[hd, :QK_NOPE_DIM, :] = q[:, hd * QK_NOPE_DIM:(hd + 1) * QK_NOPE_DIM].T.astype(BF16)

    ckv = _rms(proj[:, Q_LORA:Q_LORA + KV_LORA], gkv_ref[...]).astype(BF16)
    kv = jnp.dot(ckv, wukv_ref[...], preferred_element_type=F32)
    for hd in range(N_HEADS):
        k_ref[hd, :, :QK_NOPE_DIM] = kv[:, hd * QK_NOPE_DIM:(hd + 1) * QK_NOPE_DIM].astype(BF16)
        k_ref[hd, :, QK_NOPE_DIM:] = k_rope
        v = kv[:, nope_w + hd * V_HEAD_DIM:nope_w + (hd + 1) * V_HEAD_DIM]
        vT_ref[hd, :, :] = v.T.astype(BF16)

    u = proj[:, D_MODEL - FNET_WIDTH:].astype(BF16)
    for grp in range(FNET_GROUPS):
        z = jnp.dot(u[:, grp * FNET_GROUP_DIM:(grp + 1) * FNET_GROUP_DIM], cs_ref[...],
                    preferred_element_type=F32)
        z_ref[grp, 0] = z[:, :FNET_GROUP_DIM].reshape(tm // n2, n2, FNET_GROUP_DIM)
        z_ref[grp, 1] = z[:, FNET_GROUP_DIM:].reshape(tm // n2, n2, FNET_GROUP_DIM)


def _mixpre(x1, b, s, n1, n2, gmix, win, gq, wuq, gkv, wukv, tc, ts, cs, tm):
    t = b * s
    ns = s // tm
    kernel = functools.partial(_mixpre_kernel, n2=n2)
    return pl.pallas_call(
        kernel,
        grid=(t // tm,),
        in_specs=[
            pl.BlockSpec((tm, D_MODEL), lambda i: (i, 0)),
            _const_spec((1, D_MODEL)),
            _const_spec((D_MODEL, D_MODEL)),
            _const_spec((1, Q_LORA)),
            _const_spec((Q_LORA, 2 * MLA_WIDTH)),
            _const_spec((1, KV_LORA)),
            _const_spec((KV_LORA, 2 * MLA_WIDTH)),
            pl.BlockSpec((tm, LANES), lambda i: (i % ns, 0)),
            pl.BlockSpec((tm, LANES), lambda i: (i % ns, 0)),
            _const_spec((FNET_GROUP_DIM, 2 * FNET_GROUP_DIM)),
        ],
        out_specs=[
            pl.BlockSpec((None, N_HEADS, QK_HEAD_DIM, tm), lambda i: (i // ns, 0, 0, i % ns)),
            pl.BlockSpec((None, N_HEADS, tm, QK_HEAD_DIM), lambda i: (i // ns, 0, i % ns, 0)),
            pl.BlockSpec((None, N_HEADS, V_HEAD_DIM, tm), lambda i: (i // ns, 0, 0, i % ns)),
            pl.BlockSpec((None, FNET_GROUPS, 2, tm // n2, n2, FNET_GROUP_DIM),
                         lambda i: (i // ns, 0, 0, i % ns, 0, 0)),
        ],
        out_shape=[
            jax.ShapeDtypeStruct((b, N_HEADS, QK_HEAD_DIM, s), BF16),
            jax.ShapeDtypeStruct((b, N_HEADS, s, QK_HEAD_DIM), BF16),
            jax.ShapeDtypeStruct((b, N_HEADS, V_HEAD_DIM, s), BF16),
            jax.ShapeDtypeStruct((b, FNET_GROUPS, 2, n1, n2, FNET_GROUP_DIM), F32),
        ],
        compiler_params=pltpu.CompilerParams(
            dimension_semantics=("arbitrary",), vmem_limit_bytes=VMEM_LIMIT_BYTES),
        name="mixpre",
    )(x1, gmix, win, gq, wuq, gkv, wukv, tc, ts, cs)


def _attn_kernel(qT_ref, k_ref, vT_ref, o_ref, s_scr, p_scr, acc_scr, *, tk, unroll):
    tq = qT_ref.shape[1]
    nk = k_ref.shape[0] // tk
    ring = s_scr.shape[0]

    def scores(c, slot):
        off = pl.multiple_of(c * tk, tk)
        s_scr[slot] = jnp.dot(k_ref[pl.ds(off, tk), :], qT_ref[...],
                              preferred_element_type=F32)

    def pv(c, slot, alpha):
        off = pl.multiple_of(c * tk, tk)
        acc_scr[...] = alpha * acc_scr[...] + jnp.dot(
            vT_ref[:, pl.ds(off, tk)], p_scr[slot], preferred_element_type=F32)

    def softmax(slot, m, l):
        m_new = jnp.maximum(m, jnp.max(s_scr[slot], axis=0, keepdims=True))
        alpha = jnp.exp2(m - m_new)
        p = jnp.exp2(s_scr[slot] - m_new)
        p_scr[slot] = p.astype(BF16)
        return m_new, alpha * l + jnp.sum(p, axis=0, keepdims=True), alpha

    def trip(t, carry):
        m, l, alpha_prev = carry
        for j in range(unroll):
            c = t * unroll + j
            slot, nxt, prv = j % ring, (j + 1) % ring, (j - 1) % ring
            scores(jnp.minimum(c + 1, nk - 1), nxt)
            pv(jnp.maximum(c - 1, 0), prv, alpha_prev)
            m, l, alpha_prev = softmax(slot, m, l)
        return m, l, alpha_prev

    acc_scr[...] = jnp.zeros_like(acc_scr)
    p_scr[ring - 1] = jnp.zeros((tk, tq), BF16)
    scores(0, 0)
    init = (jnp.full((1, tq), -1e30, F32), jnp.zeros((1, tq), F32), jnp.ones((1, tq), F32))
    _, l, alpha_last = lax.fori_loop(0, nk // unroll, trip, init)
    pv(nk - 1, (nk - 1) % ring, alpha_last)
    o_ref[...] = (acc_scr[...] / l).T.astype(o_ref.dtype)


def _attention(qT, k, vT, tq, tk):
    b, _, _, s = qT.shape
    nk = s // tk
    unroll = min(8, nk)
    ring = 2
    assert unroll % ring == 0 and nk % unroll == 0
    kernel = functools.partial(_attn_kernel, tk=tk, unroll=unroll)
    return pl.pallas_call(
        kernel,
        grid=(b, N_HEADS, s // tq),
        in_specs=[
            pl.BlockSpec((None, None, QK_HEAD_DIM, tq), lambda bi, hi, qi: (bi, hi, 0, qi)),
            pl.BlockSpec((None, None, s, QK_HEAD_DIM), lambda bi, hi, qi: (bi, hi, 0, 0)),
            pl.BlockSpec((None, None, V_HEAD_DIM, s), lambda bi, hi, qi: (bi, hi, 0, 0)),
        ],
        out_specs=pl.BlockSpec((None, tq, V_HEAD_DIM), lambda bi, hi, qi: (bi, qi, hi)),
        out_shape=jax.ShapeDtypeStruct((b, s, MLA_WIDTH), BF16),
        scratch_shapes=[
            pltpu.VMEM((ring, tk, tq), F32),
            pltpu.VMEM((ring, tk, tq), BF16),
            pltpu.VMEM((V_HEAD_DIM, tq), F32),
        ],
        compiler_params=pltpu.CompilerParams(
            dimension_semantics=("arbitrary", "arbitrary", "arbitrary"),
            vmem_limit_bytes=VMEM_LIMIT_BYTES),
        name="attn",
    )(qT, k, vT)


def _dft_kernel(z_ref, f1_ref, tr_ref, ti_ref, f2_ref, o_ref, w_scr, z2, o2, *, steps1):
    step = pl.program_id(1)
    n1, nc = z_ref.shape[1], z_ref.shape[2]
    n2, kc = o_ref.shape[0], o_ref.shape[1]
    pitch = w_scr.shape[0] // (2 * n1)
    tile = F32_SUBLANES

    @pl.when(step < steps1)
    def _():
        base = step * nc
        for grp in range(nc // tile):
            z2[grp] = z_ref[:, :, grp * tile:(grp + 1) * tile, :].reshape(2 * n1 * tile, LANES)
        for j in range(0, nc, 2):
            grp, jl = divmod(j, tile)
            slabs = [[z2[grp, pl.ds(e * n1 * tile + jl + d, n1, stride=tile), :] for d in range(2)]
                     for e in range(2)]
            rhs = jnp.concatenate([jnp.concatenate(slabs[0], axis=1),
                                   jnp.concatenate(slabs[1], axis=1)], axis=0).astype(BF16)
            g = jnp.dot(f1_ref[...], rhs, preferred_element_type=F32)
            for d in range(2):
                gr = g[:n1, d * LANES:(d + 1) * LANES]
                gi = g[n1:, d * LANES:(d + 1) * LANES]
                tr = tr_ref[j + d]
                ti = ti_ref[j + d]
                w_scr[pl.ds(base + j + d, n1, stride=pitch), :] = gr * tr - gi * ti
                w_scr[pl.ds(n1 * pitch + base + j + d, n1, stride=pitch), :] = gr * ti + gi * tr

    @pl.when(step >= steps1)
    def _():
        kb = (step - steps1) * kc
        for j in range(0, kc, 2):
            grp, jl = divmod(j, tile)
            slabs = [[w_scr[pl.ds(pl.multiple_of((e * n1 + kb + j + d) * pitch, tile), n2), :]
                      for d in range(2)] for e in range(2)]
            rhs = jnp.concatenate([jnp.concatenate(slabs[0], axis=1),
                                   jnp.concatenate(slabs[1], axis=1)], axis=0).astype(BF16)
            y = jnp.dot(f2_ref[...], rhs, preferred_element_type=F32)
            o2[grp, pl.ds(jl, n2, stride=tile), :] = y[:, :LANES]
            o2[grp, pl.ds(jl + 1, n2, stride=tile), :] = y[:, LANES:]
        for grp in range(kc // tile):
            o_ref[:, grp * tile:(grp + 1) * tile, :] = o2[grp].reshape(n2, tile, LANES)


def _dft(z, fbig, tr, ti, f2):
    bg, _, n1, n2, c = z.shape
    nc = kc = DFT_CHUNK
    steps1, steps2 = n2 // nc, n1 // kc
    pitch = n2 + DFT_PITCH_PAD
    kernel = functools.partial(_dft_kernel, steps1=steps1)
    return pl.pallas_call(
        kernel,
        grid=(bg, steps1 + steps2),
        in_specs=[
            pl.BlockSpec((None, 2, n1, nc, c),
                         lambda gi, si: (gi, 0, 0, jnp.minimum(si, steps1 - 1), 0)),
            _const_spec((2 * n1, 2 * n1)),
            pl.BlockSpec((nc, n1, c), lambda gi, si: (jnp.minimum(si, steps1 - 1), 0, 0)),
            pl.BlockSpec((nc, n1, c), lambda gi, si: (jnp.minimum(si, steps1 - 1), 0, 0)),
            _const_spec((n2, 2 * n2)),
        ],
        out_specs=pl.BlockSpec(
            (None, n2, kc, c), lambda gi, si: (gi, 0, jnp.maximum(si - steps1, 0), 0)),
        out_shape=jax.ShapeDtypeStruct((bg, n2, n1, c), F32),
        scratch_shapes=[pltpu.VMEM((2 * n1 * pitch, c), F32),
                        pltpu.VMEM((nc // F32_SUBLANES, 2 * n1 * F32_SUBLANES, c), F32),
                        pltpu.VMEM((kc // F32_SUBLANES, n2 * F32_SUBLANES, c), F32)],
        compiler_params=pltpu.CompilerParams(
            dimension_semantics=("arbitrary", "arbitrary"), vmem_limit_bytes=VMEM_LIMIT_BYTES),
        name="dft",
    )(z, fbig, tr, ti, f2)


def _post_kernel(x_ref, a_ref, f_ref, wo_ref, g2_ref, wg_ref, wu_ref, wd_ref, gf_ref, o_ref):
    four = [f_ref[grp].astype(BF16) for grp in range(FNET_GROUPS)]
    mix = jnp.concatenate([a_ref[...]] + four, axis=-1)
    x = x_ref[...] + jnp.dot(mix, wo_ref[...], preferred_element_type=F32)
    h = _rms(x, g2_ref[...]).astype(BF16)
    x = x + 0.5 * _swiglu(h, wg_ref, wu_ref, wd_ref)
    o_ref[...] = _rms(x, gf_ref[...])


def _post(x1, attn, four, wo, g2, wg, wu, wd, gf, tm):
    t = x1.shape[0]
    ns = four.shape[2] // tm
    return pl.pallas_call(
        _post_kernel,
        grid=(t // tm,),
        in_specs=[
            pl.BlockSpec((tm, D_MODEL), lambda i: (i, 0)),
            pl.BlockSpec((tm, MLA_WIDTH), lambda i: (i, 0)),
            pl.BlockSpec((None, FNET_GROUPS, tm, FNET_GROUP_DIM), lambda i: (i // ns, 0, i % ns, 0)),
            _const_spec((D_MODEL, D_MODEL)),
            _const_spec((1, D_MODEL)),
            _const_spec((D_MODEL, D_FF)),
            _const_spec((D_MODEL, D_FF)),
            _const_spec((D_FF, D_MODEL)),
            _const_spec((1, D_MODEL)),
        ],
        out_specs=pl.BlockSpec((tm, D_MODEL), lambda i: (i, 0)),
        out_shape=jax.ShapeDtypeStruct((t, D_MODEL), F32),
        compiler_params=pltpu.CompilerParams(
            dimension_semantics=("arbitrary",), vmem_limit_bytes=VMEM_LIMIT_BYTES),
        name="post",
    )(x1, attn, four, wo, g2, wg, wu, wd, gf)


def _rope_tables(s):
    inv = 1.0 / (ROPE_THETA ** (jnp.arange(0, QK_ROPE_DIM, 2, dtype=F32) / QK_ROPE_DIM))
    ang = jnp.arange(s, dtype=F32)[:, None] * inv[None, :]
    cos, sin = jnp.cos(ang), jnp.sin(ang)
    tc = jnp.tile(cos, (1, LANES // cos.shape[1]))
    ts = jnp.tile(jnp.concatenate([-sin, sin], axis=1), (1, LANES // (2 * sin.shape[1])))
    return tc, ts


def _dft_mats(n):
    k = np.arange(n)
    ang = 2.0 * np.pi * ((k[:, None] * k[None, :]) % n) / n
    return np.cos(ang) / math.sqrt(n), -np.sin(ang) / math.sqrt(n)


def _twiddle(n1, n2):
    s = n1 * n2
    prod = (jnp.arange(n2, dtype=jnp.int32)[:, None] * jnp.arange(n1, dtype=jnp.int32)[None, :]) % s
    ang = prod.astype(F32) * (2.0 * math.pi / s)
    tr = jnp.broadcast_to(jnp.cos(ang)[:, :, None], (n2, n1, LANES))
    ti = jnp.broadcast_to(-jnp.sin(ang)[:, :, None], (n2, n1, LANES))
    return tr, ti


def _split_seq(s):
    n1 = 1 << ((s.bit_length() - 1 + 1) // 2)
    n2 = s // n1
    assert n1 * n2 == s and n2 % 16 == 0 and n1 % 16 == 0
    return n1, n2


def _prep_weights(w_in, w_uq, w_ukv):
    half = QK_ROPE_DIM // 2
    o = Q_LORA + KV_LORA
    k_r = w_in[:, o:o + QK_ROPE_DIM]
    k_sw = jnp.concatenate([k_r[:, half:], k_r[:, :half]], axis=1)
    win = jnp.concatenate([w_in[:, :o], k_r, k_sw, w_in[:, o + QK_ROPE_DIM:]], axis=1)

    wq = w_uq.reshape(Q_LORA, N_HEADS, QK_HEAD_DIM)
    nope = wq[:, :, :QK_NOPE_DIM].reshape(Q_LORA, -1)
    rope = wq[:, :, QK_NOPE_DIM:]
    rope_sw = jnp.concatenate([rope[:, :, half:], rope[:, :, :half]], axis=2)
    wuq = jnp.concatenate([nope, rope.reshape(Q_LORA, -1), rope_sw.reshape(Q_LORA, -1)], axis=1)

    wkv = w_ukv.reshape(KV_LORA, N_HEADS, QK_NOPE_DIM + V_HEAD_DIM)
    wukv = jnp.concatenate([wkv[:, :, :QK_NOPE_DIM].reshape(KV_LORA, -1),
                            wkv[:, :, QK_NOPE_DIM:].reshape(KV_LORA, -1)], axis=1)
    return win.astype(BF16), wuq.astype(BF16), wukv.astype(BF16)


def _trunk(x, w, tm=512, tq=512, tk=512):
    b, s, _ = x.shape
    t = b * s
    n1, n2 = _split_seq(s)
    tm = min(tm, s)
    x2d = x.reshape(t, D_MODEL)

    x1 = _ffn1(x2d, w["g_ffn1"], w["w1_gate"], w["w1_up"], w["w1_down"], tm)

    tc, ts = _rope_tables(s)
    qT, k, vT, z = _mixpre(x1, b, s, n1, n2, w["g_mix"], w["w_in"], w["g_q"], w["w_uq"],
                           w["g_kv"], w["w_ukv"], tc, ts, w["cs"], tm)
    attn = _attention(qT, k, vT, min(tq, s), min(tk, s))

    fr1, fi1 = _dft_mats(n1)
    fbig = jnp.asarray(np.block([[fr1, -fi1], [fi1, fr1]]), F32).astype(BF16)
    fr2, fi2 = _dft_mats(n2)
    f2 = jnp.asarray(np.concatenate([fr2, -fi2], axis=1), F32).astype(BF16)
    tr, ti = _twiddle(n1, n2)
    zz = z.reshape(b * FNET_GROUPS, 2, n1, n2, FNET_GROUP_DIM)
    four = _dft(zz, fbig, tr, ti, f2).reshape(b, FNET_GROUPS, s, FNET_GROUP_DIM)

    y = _post(x1, attn.reshape(t, MLA_WIDTH), four, w["w_o"], w["g_ffn2"], w["w2_gate"],
              w["w2_up"], w["w2_down"], w["g_final"], tm)
    return y.reshape(b, s, D_MODEL)


def kernel(x_prompt, x_sample, g_ffn1, w1_gate, w1_up, w1_down, g_mix, w_in, g_q, w_uq,
           g_kv, w_ukv, w_o, g_ffn2, w2_gate, w2_up, w2_down, g_final):
    win, wuq, wukv = _prep_weights(w_in[0], w_uq[0], w_ukv[0])
    c = np.arange(FNET_GROUP_DIM)
    ang = 2.0 * np.pi * ((c[:, None] * c[None, :]) % FNET_GROUP_DIM) / FNET_GROUP_DIM
    cs = np.concatenate([np.cos(ang), -np.sin(ang)], axis=1) / math.sqrt(FNET_GROUP_DIM)
    w = {
        "g_ffn1": g_ffn1[0][None, :], "w1_gate": w1_gate[0].astype(BF16),
        "w1_up": w1_up[0].astype(BF16), "w1_down": w1_down[0].astype(BF16),
        "g_mix": g_mix[0][None, :], "w_in": win, "g_q": g_q[0][None, :], "w_uq": wuq,
        "g_kv": g_kv[0][None, :], "w_ukv": wukv, "w_o": w_o[0].astype(BF16),
        "g_ffn2": g_ffn2[0][None, :], "w2_gate": w2_gate[0].astype(BF16),
        "w2_up": w2_up[0].astype(BF16), "w2_down": w2_down[0].astype(BF16),
        "g_final": g_final[None, :], "cs": jnp.asarray(cs, F32).astype(BF16),
    }
    return _trunk(x_prompt, w), _trunk(x_sample, w)
```
